```python
import math
import functools
import jax
import jax.numpy as jnp
from jax import lax
import numpy as np

D_MODEL = 1024
BATCH = 2
SEQ = 16384
DEPTH = 1
DEC_BATCH = 128
DEC_SEQ = 1
PAST_LEN = 8192
PAGE_SIZE = 128

D_CONV = 512
CONV_W = 3
N_HEADS = 16
HEAD_DIM = 64
N_KV = 2
GROUP = N_HEADS // N_KV
QD = N_HEADS * HEAD_DIM
KVD = N_KV * HEAD_DIM
L_CMP = 32
L_SEL = 64
N_SEL = 16
WINDOW = 512
CMP_HID = 256
Q_BLOCK = 128
FORCED_SCORE = 1e9
N_BUCKETS = 32
MAX_DIST = 128
D_FF = -(-8 * D_MODEL // (3 * 256)) * 256
ALPHA = (2 * DEPTH) ** 0.25
BETA = (8 * DEPTH) ** -0.25
LN_EPS = 1e-5
NEG_INF = -1e30
IN_WIDTHS = (D_CONV, D_CONV, D_CONV, QD, KVD, KVD, KVD, KVD, KVD, KVD, 3 * N_HEADS, 2 * D_MODEL)
N_IN = sum(IN_WIDTHS)

kernel_name = 'hybrid_conv_nsa_deepnorm_step'


def layer_norm(x, g, b):
    xf = x.astype(jnp.float32)
    mu = xf.mean(-1, keepdims=True)
    var = jnp.square(xf - mu).mean(-1, keepdims=True)
    return ((xf - mu) * lax.rsqrt(var + LN_EPS) * g.astype(jnp.float32) + b.astype(jnp.float32)).astype(x.dtype)


def t5_bucket(dist):
    d = jnp.maximum(dist, 0)
    exact = N_BUCKETS // 2
    log_ratio = jnp.log(jnp.maximum(d, 1).astype(jnp.float32) / exact) / math.log(MAX_DIST / exact)
    large = jnp.minimum(exact + (log_ratio * (N_BUCKETS - exact)).astype(jnp.int32), N_BUCKETS - 1)
    return jnp.where(d < exact, d, large)


def split_in(proj):
    cuts = np.cumsum(IN_WIDTHS)[:-1].tolist()
    return jnp.split(proj, cuts, axis=-1)


def causal_conv(u_ext, w):
    T = u_ext.shape[1] - (CONV_W - 1)
    return sum(w[k] * u_ext[:, k:k + T] for k in range(CONV_W))


def compress(k, w1, w2, pos):
    B, L = k.shape[:2]
    n = L // L_CMP
    blocks = k[:, :n * L_CMP].reshape(B, n, L_CMP, N_KV, HEAD_DIM) + pos[None, None, :, None, :]
    flat = blocks.transpose(0, 1, 3, 2, 4).reshape(B, n, N_KV, L_CMP * HEAD_DIM)
    return jax.nn.gelu(flat @ w1) @ w2


def nsa_core(q, tq, kc, vc, gather_sel, n_blocks, kw, vw, w_pos, gates, bias_tab):
    B, Q = q.shape[:2]
    qg = q.reshape(B, Q, N_KV, GROUP, HEAD_DIM)
    scale = HEAD_DIM ** -0.5
    bias_g = bias_tab.astype(jnp.float32).reshape(N_BUCKETS, N_KV, GROUP)

    nc = kc.shape[1]
    dist_c = tq[:, None] - (jnp.arange(nc) * L_CMP + (L_CMP - 1))[None, :]
    mask_c = (dist_c >= 0)[:, None, None, :]
    s_c = jnp.einsum('bqgrd,bcgd->bqgrc', qg, kc).astype(jnp.float32) * scale + bias_g[t5_bucket(dist_c)].transpose(0, 2, 3, 1)
    p_c = jnp.where(mask_c, jax.nn.softmax(jnp.where(mask_c, s_c, NEG_INF), axis=-1), 0.0)
    o_c = jnp.einsum('bqgrc,bcgd->bqgrd', p_c.astype(vc.dtype), vc)

    ratio = L_SEL // L_CMP
    imp = jnp.pad(p_c.sum(3), ((0, 0), (0, 0), (0, 0), (0, n_blocks * ratio - nc)))
    imp = imp.reshape(B, Q, N_KV, n_blocks, ratio).sum(-1)
    blk = jnp.arange(n_blocks)
    forced = (blk[None, :] == (tq // L_SEL)[:, None]) | (blk[None, :] == 0)
    started = blk[None, :] * L_SEL <= tq[:, None]
    imp = jnp.where(forced[None, :, None, :], FORCED_SCORE, jnp.where(started[None, :, None, :], imp, -1.0))
    _, idx = lax.top_k(imp, min(N_SEL, n_blocks))
    pos_s = (idx[..., None] * L_SEL + jnp.arange(L_SEL)).reshape(B, Q, N_KV, -1)
    ks, vs = gather_sel(pos_s)
    dist_s = tq[None, :, None, None] - pos_s
    bias_s = bias_g.transpose(1, 0, 2)[jnp.arange(N_KV)[None, None, :, None], t5_bucket(dist_s)]
    s_s = jnp.einsum('bqgrd,bqgkd->bqgrk', qg, ks).astype(jnp.float32) * scale + bias_s.transpose(0, 1, 2, 4, 3)
    mask_s = (dist_s >= 0)[:, :, :, None, :]
    p_s = jax.nn.softmax(jnp.where(mask_s, s_s, NEG_INF), axis=-1)
    o_s = jnp.einsum('bqgrk,bqgkd->bqgrd', p_s.astype(vs.dtype), vs)

    dist_w = tq[:, None] - w_pos[None, :]
    mask_w = ((dist_w >= 0) & (dist_w <= WINDOW) & (w_pos >= 0)[None, :])[:, None, None, :]
    s_w = jnp.einsum('bqgrd,bkgd->bqgrk', qg, kw).astype(jnp.float32) * scale + bias_g[t5_bucket(dist_w)].transpose(0, 2, 3, 1)
    p_w = jax.nn.softmax(jnp.where(mask_w, s_w, NEG_INF), axis=-1)
    o_w = jnp.einsum('bqgrk,bkgd->bqgrd', p_w.astype(vw.dtype), vw)

    o = gates[..., 0:1] * o_c + gates[..., 1:2] * o_s + gates[..., 2:3] * o_w
    return o.reshape(B, Q, QD)


def nsa_prompt(q, kcm, vcm, ksl, vsl, kwn, vwn, gates, cmp_w, bias_tab):
    B, S = q.shape[:2]
    wk1, wk2, pk, wv1, wv2, pv = cmp_w
    kc = compress(kcm, wk1, wk2, pk)
    vc = compress(vcm, wv1, wv2, pv)
    n_blocks = -(-S // L_SEL)
    pad_w = ((0, 0), (WINDOW, 0), (0, 0), (0, 0))
    kwp, vwp = jnp.pad(kwn, pad_w), jnp.pad(vwn, pad_w)
    bi = jnp.arange(B)[:, None, None, None]
    gi = jnp.arange(N_KV)[None, None, :, None]

    def gather_sel(pos):
        return ksl[bi, pos, gi], vsl[bi, pos, gi]

    def block(i):
        s0 = i * Q_BLOCK
        tq = s0 + jnp.arange(Q_BLOCK)
        sl = lambda t, n: lax.dynamic_slice_in_dim(t, s0, n, axis=1)
        w_pos = s0 - WINDOW + jnp.arange(WINDOW + Q_BLOCK)
        return nsa_core(sl(q, Q_BLOCK), tq, kc, vc, gather_sel, n_blocks,
                        sl(kwp, WINDOW + Q_BLOCK), sl(vwp, WINDOW + Q_BLOCK), w_pos, sl(gates, Q_BLOCK), bias_tab)

    o = lax.map(block, jnp.arange(S // Q_BLOCK))
    keep = min(WINDOW, S)
    return o.transpose(1, 0, 2, 3).reshape(B, S, QD), kwn[:, S - keep:], vwn[:, S - keep:]


def nsa_sample(q, kcm, vcm, ksl, vsl, kwn, vwn, gates, cmp_w, bias_tab, pools, layer, page_table, win_k, win_v):
    DB, DS = q.shape[:2]
    wk1, wk2, pk, wv1, wv2, pv = cmp_w
    pool_k_cmp, pool_v_cmp, pool_k_slc, pool_v_slc = pools
    n_pages = page_table.shape[1]
    P = n_pages * PAGE_SIZE
    past = lambda pool: pool[layer, page_table].reshape(DB, P, N_KV, HEAD_DIM)
    kc = compress(jnp.concatenate([past(pool_k_cmp).astype(kcm.dtype), kcm], axis=1), wk1, wk2, pk)
    vc = compress(jnp.concatenate([past(pool_v_cmp).astype(vcm.dtype), vcm], axis=1), wv1, wv2, pv)
    tq = P + jnp.arange(DS)
    n_blocks = -(-(P + DS) // L_SEL)
    n_new = -(-DS // L_SEL) * L_SEL
    pad_new = ((0, 0), (0, n_new - DS), (0, 0), (0, 0))
    k_new, v_new = jnp.pad(ksl, pad_new), jnp.pad(vsl, pad_new)
    bi = jnp.arange(DB)[:, None, None, None]
    gi = jnp.arange(N_KV)[None, None, :, None]

    def gather_sel(pos):
        in_past = (pos < P)[..., None]
        page = page_table[bi, jnp.minimum(pos // PAGE_SIZE, n_pages - 1)]
        off = pos % PAGE_SIZE
        j = jnp.clip(pos - P, 0, n_new - 1)
        ks = jnp.where(in_past, pool_k_slc[layer, page, off, gi].astype(k_new.dtype), k_new[bi, j, gi])
        vs = jnp.where(in_past, pool_v_slc[layer, page, off, gi].astype(v_new.dtype), v_new[bi, j, gi])
        return ks, vs

    wb = win_k.shape[1]
    kw = jnp.concatenate([win_k.astype(kwn.dtype), kwn], axis=1)
    vw = jnp.concatenate([win_v.astype(vwn.dtype), vwn], axis=1)
    w_pos = P - wb + jnp.arange(wb + DS)
    o = nsa_core(q, tq, kc, vc, gather_sel, n_blocks, kw, vw, w_pos, gates, bias_tab)
    return o, kw[:, DS:], vw[:, DS:]


def layer_forward(x, conv_prefix, attend, w_in, conv_w, w_up_conv, w_up_attn, w_o, ln1_g, ln1_b,
                  w_gate_ffn, w_up_ffn, w_down_ffn, ln2_g, ln2_b):
    B, T = x.shape[:2]
    proj = jnp.einsum('btd,dn->btn', x, w_in)
    cb, cc, ch, q, kcm, vcm, ksl, vsl, kwn, vwn, ng, mg = split_in(proj)
    u = cb * ch
    u_ext = jnp.concatenate([conv_prefix.astype(u.dtype), u], axis=1)
    z_a = cc * causal_conv(u_ext, conv_w)
    kcm, vcm, ksl, vsl, kwn, vwn = (t.reshape(B, T, N_KV, HEAD_DIM) for t in (kcm, vcm, ksl, vsl, kwn, vwn))
    gates = jax.nn.sigmoid(ng).reshape(B, T, N_KV, GROUP, 3)
    o_b, win_k, win_v = attend(q, kcm, vcm, ksl, vsl, kwn, vwn, gates)
    g_a, g_b = jnp.split(jax.nn.sigmoid(mg), 2, axis=-1)
    mix = (g_a * (z_a @ w_up_conv) + g_b * (o_b @ w_up_attn)) @ w_o
    h = layer_norm(ALPHA * x + mix, ln1_g, ln1_b)
    ffn = (jax.nn.silu(h @ w_gate_ffn) * (h @ w_up_ffn)) @ w_down_ffn
    y = layer_norm(ALPHA * h + ffn, ln2_g, ln2_b)
    return y, (kcm, vcm, ksl, vsl, win_k, win_v, u_ext[:, -(CONV_W - 1):])


def stack_state(states, i):
    return jnp.stack([s[i] for s in states])


def setup_inputs(seed: int = 0) -> dict:
    key = jax.random.key(seed)
    keys = iter(jax.random.split(key, 40))

    def nrm(shape, scale=1.0):
        return jax.random.normal(next(keys), shape, jnp.float32) * scale

    n_pages = PAST_LEN // PAGE_SIZE
    n_pool = (DEC_BATCH * n_pages * 5) // 4
    win_buf = min(WINDOW, PAST_LEN)
    pool_shape = (DEPTH, n_pool, PAGE_SIZE, N_KV, HEAD_DIM)
    page_table = jax.random.permutation(next(keys), n_pool)[:DEC_BATCH * n_pages].reshape(DEC_BATCH, n_pages).astype(jnp.int32)
    return dict(
        x_prompt=nrm((BATCH, SEQ, D_MODEL)),
        x_sample=nrm((DEC_BATCH, DEC_SEQ, D_MODEL)),
        cache_k_cmp=nrm(pool_shape),
        cache_v_cmp=nrm(pool_shape),
        cache_k_slc=nrm(pool_shape),
        cache_v_slc=nrm(pool_shape),
        state_k_win=nrm((DEPTH, DEC_BATCH, win_buf, N_KV, HEAD_DIM)),
        state_v_win=nrm((DEPTH, DEC_BATCH, win_buf, N_KV, HEAD_DIM)),
        state_conv=nrm((DEPTH, DEC_BATCH, CONV_W - 1, D_CONV)),
        page_table=page_table,
        w_in=nrm((DEPTH, D_MODEL, N_IN), D_MODEL ** -0.5),
        conv_w=nrm((DEPTH, CONV_W, D_CONV), CONV_W ** -0.5),
        w_cmp_k1=nrm((DEPTH, L_CMP * HEAD_DIM, CMP_HID), (L_CMP * HEAD_DIM) ** -0.5),
        w_cmp_k2=nrm((DEPTH, CMP_HID, HEAD_DIM), CMP_HID ** -0.5),
        pos_cmp_k=nrm((DEPTH, L_CMP, HEAD_DIM), 0.1),
        w_cmp_v1=nrm((DEPTH, L_CMP * HEAD_DIM, CMP_HID), (L_CMP * HEAD_DIM) ** -0.5),
        w_cmp_v2=nrm((DEPTH, CMP_HID, HEAD_DIM), CMP_HID ** -0.5),
        pos_cmp_v=nrm((DEPTH, L_CMP, HEAD_DIM), 0.1),
        rel_bias=nrm((N_BUCKETS, N_HEADS), 0.5),
        w_up_conv=nrm((DEPTH, D_CONV, D_MODEL), BETA * D_CONV ** -0.5),
        w_up_attn=nrm((DEPTH, QD, D_MODEL), BETA * QD ** -0.5),
        w_o=nrm((DEPTH, D_MODEL, D_MODEL), BETA * D_MODEL ** -0.5),
        ln1_g=1.0 + nrm((DEPTH, D_MODEL), 0.02),
        ln1_b=nrm((DEPTH, D_MODEL), 0.02),
        w_gate_ffn=nrm((DEPTH, D_MODEL, D_FF), D_MODEL ** -0.5),
        w_up_ffn=nrm((DEPTH, D_MODEL, D_FF), BETA * D_MODEL ** -0.5),
        w_down_ffn=nrm((DEPTH, D_FF, D_MODEL), BETA * D_FF ** -0.5),
        ln2_g=1.0 + nrm((DEPTH, D_MODEL), 0.02),
        ln2_b=nrm((DEPTH, D_MODEL), 0.02),
    )


def reference(x_prompt, x_sample, cache_k_cmp, cache_v_cmp, cache_k_slc, cache_v_slc, state_k_win, state_v_win,
              state_conv, page_table, w_in, conv_w, w_cmp_k1, w_cmp_k2, pos_cmp_k, w_cmp_v1, w_cmp_v2, pos_cmp_v,
              rel_bias, w_up_conv, w_up_attn, w_o, ln1_g, ln1_b, w_gate_ffn, w_up_ffn, w_down_ffn, ln2_g, ln2_b):
    x_p, x_s = x_prompt, x_sample
    new_p, new_s = [], []
    for l in range(DEPTH):
        lw = (w_in[l], conv_w[l], w_up_conv[l], w_up_attn[l], w_o[l], ln1_g[l], ln1_b[l],
              w_gate_ffn[l], w_up_ffn[l], w_down_ffn[l], ln2_g[l], ln2_b[l])
        cmp_w = (w_cmp_k1[l], w_cmp_k2[l], pos_cmp_k[l], w_cmp_v1[l], w_cmp_v2[l], pos_cmp_v[l])
        attend_p = functools.partial(nsa_prompt, cmp_w=cmp_w, bias_tab=rel_bias)
        prefix = jnp.zeros((x_p.shape[0], CONV_W - 1, D_CONV), x_p.dtype)
        x_p, st_p = layer_forward(x_p, prefix, attend_p, *lw)
        attend_s = functools.partial(nsa_sample, cmp_w=cmp_w, bias_tab=rel_bias,
                                     pools=(cache_k_cmp, cache_v_cmp, cache_k_slc, cache_v_slc), layer=l,
                                     page_table=page_table, win_k=state_k_win[l], win_v=state_v_win[l])
        x_s, st_s = layer_forward(x_s, state_conv[l], attend_s, *lw)
        new_p.append(st_p)
        new_s.append(st_s)
    return (x_p, x_s,
            stack_state(new_p, 0), stack_state(new_p, 1), stack_state(new_p, 2), stack_state(new_p, 3),
            stack_state(new_p, 4), stack_state(new_p, 5), stack_state(new_p, 6),
            stack_state(new_s, 0), stack_state(new_s, 1), stack_state(new_s, 2), stack_state(new_s, 3),
            stack_state(new_s, 4), stack_state(new_s, 5), stack_state(new_s, 6))
```

```python
import functools
import math

import numpy as np
import jax
import jax.numpy as jnp
from jax import lax
from jax.experimental import pallas as pl
from jax.experimental.pallas import tpu as pltpu

F32 = jnp.float32
BF16 = jnp.bfloat16

D_MODEL = 1024
D_CONV = 512
CONV_W = 3
N_HEADS = 16
HEAD_DIM = 64
N_KV = 2
GROUP = N_HEADS // N_KV
QD = N_HEADS * HEAD_DIM
KVD = N_KV * HEAD_DIM
L_CMP = 32
L_SEL = 64
N_SEL = 16
WINDOW = 512
CMP_HID = 256
FORCED_SCORE = 1e9
N_BUCKETS = 32
MAX_DIST = 128
D_FF = -(-8 * D_MODEL // (3 * 256)) * 256
DEPTH = 1
ALPHA = (2 * DEPTH) ** 0.25
LN_EPS = 1e-5
NEG = -1e30
PAGE_SIZE = 128

LANES = 128
VMEM_LIMIT = 56 * 1024 * 1024

C_CB, C_CC, C_CH, C_Q, C_KV, C_NG, C_MG, C_END = 0, 512, 1024, 1536, 2560, 3328, 3456, 5504

TQ = 256
KT = 256
HALF_KEYS = 8192


def _dot(a, b):
    return jnp.dot(a, b, preferred_element_type=F32)


def _dot_nt(a, b):
    return lax.dot_general(a, b, (((1,), (1,)), ((), ())), preferred_element_type=F32)


def _cparams(sem):
    return pltpu.CompilerParams(dimension_semantics=sem, vmem_limit_bytes=VMEM_LIMIT)


def _layer_norm(x, g, b):
    mu = jnp.mean(x, axis=-1, keepdims=True)
    xc = x - mu
    var = jnp.mean(xc * xc, axis=-1, keepdims=True)
    return xc * lax.rsqrt(var + LN_EPS) * g + b


def _bucket_table():
    d = np.arange(0, MAX_DIST + 1)
    exact = N_BUCKETS // 2
    log_ratio = (np.log(np.maximum(d, 1).astype(np.float32) / np.float32(exact))
                 / np.float32(math.log(MAX_DIST / exact))).astype(np.float32)
    large = np.minimum(exact + (log_ratio * np.float32(N_BUCKETS - exact)).astype(np.int32), N_BUCKETS - 1)
    return np.where(d < exact, d, large).astype(np.int32)


def _bias_from_dist(bias_p, dist, causal):
    dc = np.clip(dist, 0, MAX_DIST)
    val = jnp.moveaxis(bias_p[dc], -1, 0)
    if causal:
        val = jnp.where(jnp.asarray(dist < 0), NEG, val)
    return val


def _proj_kernel(*refs, tm, sample):
    if sample:
        (x_ref, w_ref, cw_ref, p0_ref, p1_ref, za_ref, q_ref, kcm_ref, vcm_ref, ksl_ref, vsl_ref, kwn_ref,
         vwn_ref, gates_ref, ga_ref, gb_ref, u_ref) = refs
    else:
        (x_ref, w_ref, cw_ref, za_ref, q_ref, kcm_ref, vcm_ref, ksl_ref, vsl_ref, kwn_ref, vwn_ref,
         kslb_ref, vslb_ref, kwnb_ref, vwnb_ref, gates_ref, ga_ref, gb_ref, conv_ref, ubuf) = refs
    xb = x_ref[...].astype(BF16)

    def mm(c0, c1):
        return _dot(xb, w_ref[:, c0:c1])

    u = mm(C_CB, C_CC) * mm(C_CH, C_Q)
    cc = mm(C_CC, C_CH)
    cw = cw_ref[...]
    if sample:
        u1 = p1_ref[...]
        u2 = p0_ref[...]
        u_ref[...] = u
    else:
        @pl.when(pl.program_id(1) == 0)
        def _():
            ubuf[0:8, :] = jnp.zeros((8, D_CONV), F32)
        ubuf[8:tm + 8, :] = u
        u1 = ubuf[7:tm + 7, :]
        u2 = ubuf[6:tm + 6, :]
        conv_ref[...] = ubuf[tm + 6:tm + 8, :]
        ubuf[0:8, :] = ubuf[tm:tm + 8, :]
    za_ref[...] = (cc * (cw[0:1] * u2 + cw[1:2] * u1 + cw[2:3] * u)).astype(BF16)
    q_ref[...] = mm(C_Q, C_KV).astype(BF16)
    kv = mm(C_KV, C_NG)
    for i, r in enumerate((kcm_ref, vcm_ref, ksl_ref, vsl_ref, kwn_ref, vwn_ref)):
        r[...] = kv[:, i * KVD:(i + 1) * KVD]
    if not sample:
        for i, r in enumerate((kslb_ref, vslb_ref, kwnb_ref, vwnb_ref)):
            r[...] = kv[:, (i + 2) * KVD:(i + 3) * KVD].astype(BF16)
    gates_ref[...] = jax.nn.sigmoid(mm(C_NG, C_MG))
    mg = jax.nn.sigmoid(mm(C_MG, C_END))
    ga_ref[...] = mg[:, :D_MODEL]
    gb_ref[...] = mg[:, D_MODEL:]


def _proj_call(x, w_perm, conv_w, prefix, tm):
    B, T, _ = x.shape
    sample = prefix is not None
    nt = T // tm
    row = lambda w: pl.BlockSpec((None, tm, w), lambda b, s: (b, s, 0))
    full = lambda a: pl.BlockSpec(a.shape, lambda b, s: (0,) * a.ndim)
    sds = lambda w, dt: jax.ShapeDtypeStruct((B, T, w), dt)
    in_specs = [row(D_MODEL), full(w_perm), full(conv_w)]
    args = [x, w_perm, conv_w]
    out_shape = [sds(D_CONV, BF16), sds(QD, BF16)] + [sds(KVD, F32)] * 6
    out_specs = [row(D_CONV), row(QD)] + [row(KVD)] * 6
    scratch = []
    if sample:
        in_specs += [row(D_CONV), row(D_CONV)]
        args += list(prefix)
    else:
        out_shape += [sds(KVD, BF16)] * 4
        out_specs += [row(KVD)] * 4
    out_shape += [sds(LANES, F32), sds(D_MODEL, F32), sds(D_MODEL, F32)]
    out_specs += [row(LANES), row(D_MODEL), row(D_MODEL)]
    if sample:
        out_shape.append(sds(D_CONV, F32))
        out_specs.append(row(D_CONV))
    else:
        out_shape.append(jax.ShapeDtypeStruct((B, CONV_W - 1, D_CONV), F32))
        out_specs.append(pl.BlockSpec((None, CONV_W - 1, D_CONV), lambda b, s: (b, 0, 0)))
        scratch.append(pltpu.VMEM((tm + 8, D_CONV), F32))
    return pl.pallas_call(
        functools.partial(_proj_kernel, tm=tm, sample=sample),
        grid=(B, nt), in_specs=in_specs, out_specs=out_specs, out_shape=out_shape,
        scratch_shapes=scratch, compiler_params=_cparams(("arbitrary", "arbitrary")),
    )(*args)


def _compress_rows(x, pos_ref, w1_ref, w2_ref, tmp_ref):
    xb = (x + pos_ref[...]).astype(BF16)
    hid = jax.nn.gelu(_dot(xb, w1_ref[...]))
    tmp_ref[...] = _dot(hid.astype(BF16), w2_ref[...])
    return tmp_ref[pl.ds(0, 128, stride=2), :], tmp_ref[pl.ds(1, 128, stride=2), :]


def _compress_kernel(x_ref, pos_ref, w1_ref, w2_ref, o_ref, tmp_ref):
    ev, od = _compress_rows(x_ref[...], pos_ref, w1_ref, w2_ref, tmp_ref)
    o_ref[0] = ev.astype(BF16)
    o_ref[1] = od.astype(BF16)


def _compress_prompt(rows, pos2, w1p, w2p, B):
    nchunk = rows.shape[0] // (256 * B)
    full = lambda a: pl.BlockSpec(a.shape, lambda b, i: (0,) * a.ndim)
    return pl.pallas_call(
        _compress_kernel, grid=(B, nchunk),
        in_specs=[pl.BlockSpec((256, 4096), lambda b, i: (b * nchunk + i, 0)), full(pos2), full(w1p), full(w2p)],
        out_specs=pl.BlockSpec((None, 2, 128, KVD), lambda b, i: (b, 0, i, 0)),
        out_shape=jax.ShapeDtypeStruct((B, 2, nchunk * 128, KVD), BF16),
        scratch_shapes=[pltpu.VMEM((256, KVD), F32)],
        compiler_params=_cparams(("arbitrary", "arbitrary")),
    )(rows, pos2, w1p, w2p)


def _compress_pages_kernel(pt_ref, pool_ref, pos_ref, w1_ref, w2_ref, o_ref, buf, tmp_ref, sem, *, n_pages):
    b = pl.program_id(0)
    nb = pl.num_programs(0)

    def copy(bb, p, slot):
        page = pt_ref[bb * n_pages + p]
        return pltpu.make_async_copy(pool_ref.at[page], buf.at[slot, pl.ds(4 * p, 4), :], sem.at[slot])

    def start_all(bb, slot):
        for p in range(n_pages):
            copy(bb, p, slot).start()

    slot = lax.rem(b, 2)

    @pl.when(b == 0)
    def _():
        start_all(b, 0)

    for p in range(n_pages):
        copy(b, p, slot).wait()

    @pl.when(b + 1 < nb)
    def _():
        start_all(b + 1, 1 - slot)

    ev, od = _compress_rows(buf[slot], pos_ref, w1_ref, w2_ref, tmp_ref)
    o_ref[0] = ev.astype(BF16)
    o_ref[1] = od.astype(BF16)


def _compress_pages(page_table, pool_rows, pos2, w1p, w2p):
    DB, n_pages = page_table.shape
    assert n_pages * 4 == 256
    full = lambda a: pl.BlockSpec(a.shape, lambda b, pt: (0,) * a.ndim)
    return pl.pallas_call(
        functools.partial(_compress_pages_kernel, n_pages=n_pages),
        grid_spec=pltpu.PrefetchScalarGridSpec(
            num_scalar_prefetch=1, grid=(DB,),
            in_specs=[pl.BlockSpec(memory_space=pl.ANY), full(pos2), full(w1p), full(w2p)],
            out_specs=pl.BlockSpec((None, 2, 128, KVD), lambda b, pt: (b, 0, 0, 0)),
            scratch_shapes=[pltpu.VMEM((2, 256, L_CMP * KVD), F32), pltpu.VMEM((256, KVD), F32),
                            pltpu.SemaphoreType.DMA((2,))]),
        out_shape=jax.ShapeDtypeStruct((DB, 2, 128, KVD), BF16),
        compiler_params=_cparams(("arbitrary",)),
    )(page_table.reshape(-1), pool_rows, pos2, w1p, w2p)


def _topk_extract(v, k, want_idx):
    lanef = lax.broadcasted_iota(jnp.int32, v.shape, 1).astype(F32)
    idx = []
    for _ in range(k):
        mx = jnp.max(v, axis=-1, keepdims=True)
        first = jnp.min(jnp.where(v == mx, lanef, 1e9), axis=-1, keepdims=True)
        v = jnp.where(lanef == first, -jnp.inf, v)
        if want_idx:
            idx.append(first)
    return v, idx


def _cmp_kernel(q_ref, kc_ref, vc_ref, gates_ref, nb_ref, oc_ref, m_ref, *, nc):
    t0 = pl.program_id(1) * TQ
    half = nc // 2
    lane = lax.broadcasted_iota(jnp.int32, (nc, LANES), 1)
    crow = lax.broadcasted_iota(jnp.int32, (nc, LANES), 0)
    cnat = jnp.where(crow < half, 2 * crow, 2 * (crow - half) + 1)
    cbase = t0 // L_CMP - 8
    e = jnp.where(lane < 16, lane, lane - 16)
    ext = jnp.where((lane < 32) & (cnat == cbase + e), 1.0, 0.0)
    ext = jnp.where((lane == 32) & (cnat >= cbase + 16), NEG, ext).astype(BF16)
    lo = lane < HEAD_DIM
    kc = kc_ref[...]
    vc = vc_ref[...]
    zero = jnp.zeros_like(kc)
    rhs = [jnp.concatenate([jnp.where(lo, kc, zero), ext], axis=1),
           jnp.concatenate([jnp.where(lo, zero, kc), ext], axis=1)]
    vg = [jnp.where(lo, vc, zero), jnp.where(lo, zero, vc)]
    gates = gates_ref[...]
    imp = [jnp.zeros((TQ, nc), F32), jnp.zeros((TQ, nc), F32)]
    for j in range(GROUP):
        qs = q_ref[:, j * LANES:(j + 1) * LANES]
        osl = jnp.zeros((TQ, LANES), F32)
        for g in range(N_KV):
            hh = 2 * j + g
            lhs = jnp.concatenate([qs, nb_ref[hh]], axis=1)
            s = _dot_nt(lhs, rhs[g])
            m = jnp.max(s, axis=-1, keepdims=True)
            ex = jnp.exp(s - m)
            l = jnp.sum(ex, axis=-1, keepdims=True)
            p = ex * jnp.where(m > 0.1 * NEG, 1.0 / l, 0.0)
            imp[g] = imp[g] + p
            osl = osl + gates[:, hh:hh + 1] * _dot(p.astype(BF16), vg[g])
        oc_ref[:, j * LANES:(j + 1) * LANES] = osl
    blk = lax.broadcasted_iota(jnp.int32, (TQ, half), 1)
    trow = t0 + lax.broadcasted_iota(jnp.int32, (TQ, half), 0)
    forced = (blk == trow // L_SEL) | (blk == 0)
    started = blk * L_SEL <= trow
    for g in range(N_KV):
        v = imp[g][:, :half] + imp[g][:, half:]
        v = jnp.where(forced, FORCED_SCORE, jnp.where(started, v, -1.0))
        v, _ = _topk_extract(v, N_SEL, False)
        m_ref[:, g * half:(g + 1) * half] = jnp.where(v == -jnp.inf, 0.0, NEG).astype(BF16)


def _cmp_call(q, kc, vc, gates, nbt):
    B, S, _ = q.shape
    nc = kc.shape[1]
    row = lambda w: pl.BlockSpec((None, TQ, w), lambda b, t: (b, t, 0))
    perb = lambda a: pl.BlockSpec((None,) + a.shape[1:], lambda b, t: (b,) + (0,) * (a.ndim - 1))
    full = lambda a: pl.BlockSpec(a.shape, lambda b, t: (0,) * a.ndim)
    return pl.pallas_call(
        functools.partial(_cmp_kernel, nc=nc), grid=(B, S // TQ),
        in_specs=[row(QD), perb(kc), perb(vc), row(LANES), full(nbt)],
        out_specs=[row(QD), row(nc)],
        out_shape=[jax.ShapeDtypeStruct((B, S, QD), F32), jax.ShapeDtypeStruct((B, S, nc), BF16)],
        compiler_params=_cparams(("arbitrary", "arbitrary")),
    )(q, kc, vc, gates, nbt)


def _sel_kernel(q_ref, m_ref, gates_ref, k_ref, v_ref, oh_ref, bt_ref, os_ref, lhs_s, acc_s, m_s, l_s, *, nh):
    qt = pl.program_id(1)
    tph = HALF_KEYS // KT
    for j in range(GROUP):
        qs = q_ref[:, j * LANES:(j + 1) * LANES]
        for h in range(nh):
            for g in range(N_KV):
                c0 = (g * nh + h) * LANES
                lhs_s[(h * 2 + g) * GROUP + j] = jnp.concatenate([qs, m_ref[:, c0:c0 + LANES]], axis=1)
    m_s[...] = jnp.full(m_s.shape, NEG, F32)
    l_s[...] = jnp.zeros(l_s.shape, F32)
    acc_s[...] = jnp.zeros(acc_s.shape, F32)
    lo_k = lax.broadcasted_iota(jnp.int32, (KT, LANES), 1) < HEAD_DIM
    lo_q = lax.broadcasted_iota(jnp.int32, (TQ, LANES), 1) < HEAD_DIM

    def tile(kt, kind):
        k_t = k_ref[pl.ds(pl.multiple_of(kt * KT, KT), KT), :]
        v_t = v_ref[pl.ds(pl.multiple_of(kt * KT, KT), KT), :]
        h = kt // tph
        oh = oh_ref[pl.ds(pl.multiple_of((kt - h * tph) * KT, KT), KT), :]
        zero = jnp.zeros_like(k_t)
        rhs = [jnp.concatenate([jnp.where(lo_k, k_t, zero), oh], axis=1),
               jnp.concatenate([jnp.where(lo_k, zero, k_t), oh], axis=1)]
        vg = [jnp.where(lo_k, v_t, zero), jnp.where(lo_k, zero, v_t)]
        for j in range(GROUP):
            pv = []
            al = []
            for g in range(N_KV):
                hh = 2 * j + g
                s = _dot_nt(lhs_s[(h * 2 + g) * GROUP + j], rhs[g])
                if kind:
                    s = s + bt_ref[kind - 1, hh].astype(F32)
                mp = m_s[hh]
                mn = jnp.maximum(mp, jnp.max(s, axis=-1, keepdims=True))
                alpha = jnp.exp(mp - mn)
                p = jnp.exp(s - mn)
                l_s[hh] = alpha * l_s[hh] + jnp.sum(p, axis=-1, keepdims=True)
                m_s[hh] = mn
                pv.append(_dot(p.astype(BF16), vg[g]))
                al.append(alpha)
            sl = slice(j * LANES, (j + 1) * LANES)
            acc_s[:, sl] = acc_s[:, sl] * jnp.where(lo_q, al[0], al[1]) + pv[0] + pv[1]

    tile(qt, 2)

    @pl.when(qt >= 1)
    def _():
        tile(qt - 1, 1)

    def body(kt, c):
        tile(kt, 0)
        return c

    lax.fori_loop(0, jnp.maximum(qt - 1, 0), body, 0)

    gates = gates_ref[...]
    for j in range(GROUP):
        sc = [gates[:, N_HEADS + 2 * j + g:N_HEADS + 2 * j + g + 1] / l_s[2 * j + g] for g in range(N_KV)]
        sl = slice(j * LANES, (j + 1) * LANES)
        os_ref[:, sl] = acc_s[:, sl] * jnp.where(lo_q, sc[0], sc[1])


def _sel_call(q, msk, gates, kb, vb, onehot, btab):
    B, S, _ = q.shape
    nh = msk.shape[2] // (2 * LANES)
    row = lambda w: pl.BlockSpec((None, TQ, w), lambda b, t: (b, t, 0))
    perb = lambda a: pl.BlockSpec((None,) + a.shape[1:], lambda b, t: (b,) + (0,) * (a.ndim - 1))
    full = lambda a: pl.BlockSpec(a.shape, lambda b, t: (0,) * a.ndim)
    return pl.pallas_call(
        functools.partial(_sel_kernel, nh=nh), grid=(B, S // TQ),
        in_specs=[row(QD), row(msk.shape[2]), row(LANES), perb(kb), perb(vb), full(onehot), full(btab)],
        out_specs=row(QD),
        out_shape=jax.ShapeDtypeStruct((B, S, QD), F32),
        scratch_shapes=[pltpu.VMEM((nh * 2 * GROUP, TQ, 2 * LANES), BF16), pltpu.VMEM((TQ, QD), F32),
                        pltpu.VMEM((N_HEADS, TQ, 1), F32), pltpu.VMEM((N_HEADS, TQ, 1), F32)],
        compiler_params=_cparams(("arbitrary", "arbitrary")),
    )(q, msk, gates, kb, vb, onehot, btab)


def _win_kernel(q_ref, gates_ref, k0_ref, k1_ref, k2_ref, v0_ref, v1_ref, v2_ref, bt_ref, ow_ref):
    qt = pl.program_id(1)
    lo_k = lax.broadcasted_iota(jnp.int32, (3 * KT, LANES), 1) < HEAD_DIM
    kcat = jnp.concatenate([k0_ref[...], k1_ref[...], k2_ref[...]], axis=0)
    vcat = jnp.concatenate([v0_ref[...], v1_ref[...], v2_ref[...]], axis=0)
    zero = jnp.zeros_like(kcat)
    kg = [jnp.where(lo_k, kcat, zero), jnp.where(lo_k, zero, kcat)]
    vg = [jnp.where(lo_k, vcat, zero), jnp.where(lo_k, zero, vcat)]
    ii = lax.broadcasted_iota(jnp.int32, (TQ, KT), 0)
    jj = lax.broadcasted_iota(jnp.int32, (TQ, KT), 1)
    pen0 = jnp.where((jj >= ii) & (qt >= 2), 0.0, NEG)
    pen1 = jnp.where(qt >= 1, 0.0, NEG)
    gates = gates_ref[...]
    for j in range(GROUP):
        qs = q_ref[:, j * LANES:(j + 1) * LANES]
        osl = jnp.zeros((TQ, LANES), F32)
        for g in range(N_KV):
            hh = 2 * j + g
            s = _dot_nt(qs, kg[g])
            s = s + jnp.concatenate([pen0, bt_ref[0, hh].astype(F32) + pen1, bt_ref[1, hh].astype(F32)], axis=1)
            m = jnp.max(s, axis=-1, keepdims=True)
            ex = jnp.exp(s - m)
            l = jnp.sum(ex, axis=-1, keepdims=True)
            c = 2 * N_HEADS + hh
            osl = osl + (gates[:, c:c + 1] / l) * _dot(ex.astype(BF16), vg[g])
        ow_ref[:, j * LANES:(j + 1) * LANES] = osl


def _win_call(q, gates, kb, vb, btab):
    B, S, _ = q.shape
    row = lambda w: pl.BlockSpec((None, TQ, w), lambda b, t: (b, t, 0))
    back = lambda d: pl.BlockSpec((None, KT, KVD), lambda b, t: (b, jnp.maximum(t - d, 0), 0))
    full = lambda a: pl.BlockSpec(a.shape, lambda b, t: (0,) * a.ndim)
    return pl.pallas_call(
        _win_kernel, grid=(B, S // TQ),
        in_specs=[row(QD), row(LANES), back(2), back(1), back(0), back(2), back(1), back(0), full(btab)],
        out_specs=row(QD),
        out_shape=jax.ShapeDtypeStruct((B, S, QD), F32),
        compiler_params=_cparams(("arbitrary", "arbitrary")),
    )(q, gates, kb, kb, kb, vb, vb, vb, btab)


def _merge_kernel(za_ref, oc_ref, os_ref, ow_ref, ga_ref, gb_ref, x_ref, wuc_ref, wua_ref, wo_ref, g_ref, b_ref,
                  h_ref):
    a = _dot(za_ref[...], wuc_ref[...])
    ob = (oc_ref[...] + os_ref[...] + ow_ref[...]).astype(BF16)
    bm = _dot(ob, wua_ref[...])
    pre = ga_ref[...] * a + gb_ref[...] * bm
    mix = _dot(pre.astype(BF16), wo_ref[...])
    h_ref[...] = _layer_norm(ALPHA * x_ref[...] + mix, g_ref[...], b_ref[...])


def _merge_call(za, oc, os_, ow, ga, gb, x, wuc, wua, wo, g, b, tm):
    R = x.shape[0]
    row = lambda w: pl.BlockSpec((tm, w), lambda i: (i, 0))
    full = lambda a: pl.BlockSpec(a.shape, lambda i: (0,) * a.ndim)
    return pl.pallas_call(
        _merge_kernel, grid=(R // tm,),
        in_specs=[row(D_CONV), row(QD), row(QD), row(QD), row(D_MODEL), row(D_MODEL), row(D_MODEL),
                  full(wuc), full(wua), full(wo), full(g), full(b)],
        out_specs=row(D_MODEL), out_shape=jax.ShapeDtypeStruct((R, D_MODEL), F32),
        compiler_params=_cparams(("arbitrary",)),
    )(za, oc, os_, ow, ga, gb, x, wuc, wua, wo, g, b)


def _ffn_kernel(h_ref, wg_ref, wu_ref, wd_ref, g_ref, b_ref, y_ref):
    h = h_ref[...]
    hb = h.astype(BF16)
    act = jax.nn.silu(_dot(hb, wg_ref[...])) * _dot(hb, wu_ref[...])
    f = _dot(act.astype(BF16), wd_ref[...])
    y_ref[...] = _layer_norm(ALPHA * h + f, g_ref[...], b_ref[...])


def _ffn_call(h, wg, wu, wd, g, b, tm):
    R = h.shape[0]
    row = lambda w: pl.BlockSpec((tm, w), lambda i: (i, 0))
    full = lambda a: pl.BlockSpec(a.shape, lambda i: (0,) * a.ndim)
    return pl.pallas_call(
        _ffn_kernel, grid=(R // tm,),
        in_specs=[row(D_MODEL), full(wg), full(wu), full(wd), full(g), full(b)],
        out_specs=row(D_MODEL), out_shape=jax.ShapeDtypeStruct((R, D_MODEL), F32),
        compiler_params=_cparams(("arbitrary",)),
    )(h, wg, wu, wd, g, b)


def _scmp_kernel(qh_ref, kc_ref, vc_ref, gate_ref, bc_ref, oc_ref, imp_ref, *, nc):
    half = nc // 2
    s = _dot_nt(qh_ref[...], kc_ref[...]) + bc_ref[...]
    m = jnp.max(s, axis=-1, keepdims=True)
    ex = jnp.exp(s - m)
    p = ex * (1.0 / jnp.sum(ex, axis=-1, keepdims=True))
    o = _dot(p.astype(BF16), vc_ref[...]) * gate_ref[...]
    oc_ref[...] = o[0:GROUP] + o[GROUP:2 * GROUP]
    for g in range(N_KV):
        pg = jnp.sum(p[g * GROUP:(g + 1) * GROUP], axis=0, keepdims=True)
        imp_ref[g:g + 1, :] = pg[:, :half] + pg[:, half:]


def _scmp_call(qh, kc, vc, gate_c, bc):
    DB, _, nc, _ = (kc.shape[0], None, kc.shape[1], None)
    perb = lambda a: pl.BlockSpec((None,) + a.shape[1:], lambda b: (b,) + (0,) * (a.ndim - 1))
    full = lambda a: pl.BlockSpec(a.shape, lambda b: (0,) * a.ndim)
    return pl.pallas_call(
        functools.partial(_scmp_kernel, nc=nc), grid=(DB,),
        in_specs=[perb(qh), perb(kc), perb(vc), perb(gate_c), full(bc)],
        out_specs=[pl.BlockSpec((None, GROUP, LANES), lambda b: (b, 0, 0)),
                   pl.BlockSpec((None, N_KV, nc // 2), lambda b: (b, 0, 0))],
        out_shape=[jax.ShapeDtypeStruct((DB, GROUP, LANES), F32), jax.ShapeDtypeStruct((DB, N_KV, nc // 2), F32)],
        compiler_params=_cparams(("arbitrary",)),
    )(qh, kc, vc, gate_c, bc)


def _stopk_kernel(imp_ref, idx_ref, *, k):
    v = imp_ref[...]
    lane = lax.broadcasted_iota(jnp.int32, v.shape, 1)
    v = jnp.where(lane == 0, FORCED_SCORE, v)
    _, idx = _topk_extract(v, k, True)
    out = jnp.zeros(v.shape, F32)
    for i, col in enumerate(idx):
        out = jnp.where(lane == i, col, out)
    idx_ref[...] = out.astype(jnp.int32)


def _stopk_call(imp, k):
    return pl.pallas_call(
        functools.partial(_stopk_kernel, k=k),
        out_shape=jax.ShapeDtypeStruct(imp.shape, jnp.int32),
        compiler_params=pltpu.CompilerParams(vmem_limit_bytes=VMEM_LIMIT),
    )(imp)


def _ssel_kernel(idx_ref, pt_ref, q_ref, knew_ref, vnew_ref, kwnew_ref, vwnew_ref, kwcol_ref, vwcol_ref,
                 gate_s_ref, gate_w_ref, kwin_ref, vwin_ref, tn_ref, tw_ref, b0_ref, kpool_ref, vpool_ref,
                 os_ref, ow_ref, kwo_ref, vwo_ref, kbuf, vbuf, sem, *, n_pages, nsel, n_last):
    b = pl.program_id(0)

    def copies(g, s):
        n = idx_ref[(b * N_KV + g) * nsel + s]
        page = pt_ref[b * n_pages + n // 2]
        dst = pl.ds(s * PAGE_SIZE, PAGE_SIZE)
        return (pltpu.make_async_copy(kpool_ref.at[page, g], kbuf.at[g, :, dst], sem.at[0]),
                pltpu.make_async_copy(vpool_ref.at[page, g], vbuf.at[g, :, dst], sem.at[1]))

    for g in range(N_KV):
        for s in range(nsel):
            ck, cv = copies(g, s)
            ck.start()
            cv.start()

    def attend(qg, s, k_new, v_new, vt, bias_new, gate):
        s_new = jnp.sum(qg.astype(F32) * k_new.astype(BF16).astype(F32), axis=-1, keepdims=True) + bias_new
        m = jnp.maximum(jnp.max(s, axis=-1, keepdims=True), s_new)
        ex = jnp.exp(s - m)
        ex_new = jnp.exp(s_new - m)
        l = jnp.sum(ex, axis=-1, keepdims=True) + ex_new
        o = _dot_nt(ex.astype(BF16), vt.astype(BF16)) + ex_new.astype(BF16).astype(F32) * v_new.astype(BF16).astype(F32)
        return o * (gate / l)

    wb = kwin_ref.shape[-1]
    lane_w = lax.broadcasted_iota(jnp.int32, (HEAD_DIM, wb), 1)
    for g in range(N_KV):
        qg = q_ref[g]
        kw = kwin_ref[g]
        vw = vwin_ref[g]
        s = _dot(qg, kw.astype(BF16)) + tw_ref[g]
        ow_ref[g] = attend(qg, s, kwnew_ref[g], vwnew_ref[g], vw, b0_ref[g], gate_w_ref[g])
        kwo_ref[g] = jnp.where(lane_w == wb - 1, kwcol_ref[g], pltpu.roll(kw, wb - 1, 1))
        vwo_ref[g] = jnp.where(lane_w == wb - 1, vwcol_ref[g], pltpu.roll(vw, wb - 1, 1))

    for g in range(N_KV):
        for s in range(nsel):
            ck, cv = copies(g, s)
            ck.wait()
            cv.wait()

    upper = lax.broadcasted_iota(jnp.int32, (GROUP, PAGE_SIZE), 1) >= L_SEL
    keep_upper = jnp.where(upper, 0.0, NEG)
    keep_lower = jnp.where(upper, NEG, 0.0)
    for g in range(N_KV):
        qg = q_ref[g]
        pieces = []
        for s in range(nsel):
            n = idx_ref[(b * N_KV + g) * nsel + s]
            piece = jnp.where(n % 2 == 1, keep_upper, keep_lower)
            piece = piece + jnp.where(n == n_last, 1.0, 0.0) * tn_ref[0, g]
            piece = piece + jnp.where(n == n_last - 1, 1.0, 0.0) * tn_ref[1, g]
            pieces.append(piece)
        s = _dot(qg, kbuf[g].astype(BF16)) + jnp.concatenate(pieces, axis=1)
        os_ref[g] = attend(qg, s, knew_ref[g], vnew_ref[g], vbuf[g], b0_ref[g], gate_s_ref[g])


def _ssel_call(idx, page_table, qg, knew, vnew, kwnew, vwnew, kwcol, vwcol, gate_s, gate_w, kwin, vwin, tn, tw, b0,
               kpool, vpool):
    DB, n_pages = page_table.shape
    nsel = idx.shape[-1]
    wb = kwin.shape[-1]
    n_last = n_pages * (PAGE_SIZE // L_SEL) - 1
    perb = lambda a: pl.BlockSpec((None,) + a.shape[1:], lambda b, *_: (b,) + (0,) * (a.ndim - 1))
    full = lambda a: pl.BlockSpec(a.shape, lambda b, *_: (0,) * a.ndim)
    anyspec = pl.BlockSpec(memory_space=pl.ANY)
    head_o = pl.BlockSpec((None, N_KV, GROUP, HEAD_DIM), lambda b, *_: (b, 0, 0, 0))
    win_o = pl.BlockSpec((None, N_KV, HEAD_DIM, wb), lambda b, *_: (b, 0, 0, 0))
    head_s = jax.ShapeDtypeStruct((DB, N_KV, GROUP, HEAD_DIM), F32)
    win_s = jax.ShapeDtypeStruct((DB, N_KV, HEAD_DIM, wb), F32)
    return pl.pallas_call(
        functools.partial(_ssel_kernel, n_pages=n_pages, nsel=nsel, n_last=n_last),
        grid_spec=pltpu.PrefetchScalarGridSpec(
            num_scalar_prefetch=2, grid=(DB,),
            in_specs=[perb(qg), perb(knew), perb(vnew), perb(kwnew), perb(vwnew), perb(kwcol), perb(vwcol),
                      perb(gate_s), perb(gate_w), perb(kwin), perb(vwin), full(tn), full(tw), full(b0),
                      anyspec, anyspec],
            out_specs=[head_o, head_o, win_o, win_o],
            scratch_shapes=[pltpu.VMEM((N_KV, HEAD_DIM, nsel * PAGE_SIZE), F32),
                            pltpu.VMEM((N_KV, HEAD_DIM, nsel * PAGE_SIZE), F32),
                            pltpu.SemaphoreType.DMA((2,))]),
        out_shape=[head_s, head_s, win_s, win_s],
        compiler_params=_cparams(("arbitrary",)),
    )(idx.reshape(-1), page_table.reshape(-1), qg, knew, vnew, kwnew, vwnew, kwcol, vwcol, gate_s, gate_w,
      kwin, vwin, tn, tw, b0, kpool, vpool)


def _perm_w_in(w_in):
    o_q = 3 * D_CONV
    o_kv = o_q + QD
    o_ng = o_kv + 6 * KVD
    o_mg = o_ng + 3 * N_HEADS
    q = w_in[:, o_q:o_kv].reshape(D_MODEL, N_KV, GROUP, HEAD_DIM).transpose(0, 2, 1, 3).reshape(D_MODEL, QD)
    q = q * (HEAD_DIM ** -0.5)
    ng = w_in[:, o_ng:o_mg].reshape(D_MODEL, N_KV, GROUP, 3).transpose(0, 3, 2, 1).reshape(D_MODEL, 3 * N_HEADS)
    ng = jnp.pad(ng, ((0, 0), (0, LANES - 3 * N_HEADS)))
    return jnp.concatenate([w_in[:, :o_q], q, w_in[:, o_kv:o_ng], ng, w_in[:, o_mg:]], axis=1).astype(BF16)


def _perm_heads(t):
    return t.reshape(t.shape[:-1] + (N_KV, GROUP)).swapaxes(-1, -2).reshape(t.shape)


def _cmp_weights(w1, w2, pos):
    eye = jnp.eye(N_KV, dtype=F32)
    w1p = jnp.einsum('ldh,ab->ladbh', w1.reshape(L_CMP, HEAD_DIM, CMP_HID), eye)
    w1p = w1p.reshape(L_CMP * KVD, N_KV * CMP_HID).astype(BF16)
    w2p = jnp.einsum('hd,ab->ahbd', w2, eye).reshape(N_KV * CMP_HID, KVD).astype(BF16)
    pos2 = jnp.tile(pos[:, None, :], (1, N_KV, 1)).reshape(1, L_CMP * KVD)
    return pos2, w1p, w2p


def kernel(x_prompt, x_sample, cache_k_cmp, cache_v_cmp, cache_k_slc, cache_v_slc, state_k_win, state_v_win,
           state_conv, page_table, w_in, conv_w, w_cmp_k1, w_cmp_k2, pos_cmp_k, w_cmp_v1, w_cmp_v2, pos_cmp_v,
           rel_bias, w_up_conv, w_up_attn, w_o, ln1_g, ln1_b, w_gate_ffn, w_up_ffn, w_down_ffn, ln2_g, ln2_b):
    B, S, _ = x_prompt.shape
    DB = x_sample.shape[0]
    n_pages = page_table.shape[1]
    P = n_pages * PAGE_SIZE
    n_pool = cache_k_cmp.shape[1]
    wb = state_k_win.shape[2]
    assert x_sample.shape[1] == 1 and w_in.shape[0] == 1
    assert S % HALF_KEYS == 0 and P == HALF_KEYS and wb == WINDOW and S >= WINDOW
    nc = S // L_CMP
    nh = S // HALF_KEYS

    w_perm = _perm_w_in(w_in[0])
    cw = conv_w[0]
    cmp_k = _cmp_weights(w_cmp_k1[0], w_cmp_k2[0], pos_cmp_k[0])
    cmp_v = _cmp_weights(w_cmp_v1[0], w_cmp_v2[0], pos_cmp_v[0])
    wuc = w_up_conv[0].astype(BF16)
    wua = w_up_attn[0].reshape(N_KV, GROUP, HEAD_DIM, D_MODEL).transpose(1, 0, 2, 3).reshape(QD, D_MODEL).astype(BF16)
    wo = w_o[0].astype(BF16)
    wg, wu, wd = w_gate_ffn[0].astype(BF16), w_up_ffn[0].astype(BF16), w_down_ffn[0].astype(BF16)
    g1, b1, g2, b2 = ln1_g, ln1_b, ln2_g, ln2_b

    bucket = _bucket_table()
    bias_p = rel_bias[bucket] - rel_bias[N_BUCKETS - 1][None, :]
    bias_pp = _perm_heads(bias_p)
    ii = np.arange(TQ)[:, None]
    dist_c = ii - L_CMP * np.arange(16)[None, :] + 8 * L_CMP - (L_CMP - 1)
    nb = _bias_from_dist(bias_pp, dist_c, True)
    nb_hi = nb.astype(BF16)
    nb_lo = jnp.where(jnp.asarray(dist_c < 0), 0.0, nb - nb_hi.astype(F32)).astype(BF16)
    nbt = jnp.concatenate([nb_hi, nb_lo, jnp.ones((N_HEADS, TQ, 1), BF16),
                           jnp.zeros((N_HEADS, TQ, LANES - 33), BF16)], axis=-1)
    dist_d = ii - np.arange(KT)[None, :]
    btab = jnp.stack([_bias_from_dist(bias_pp, dist_d + KT, False),
                      _bias_from_dist(bias_pp, dist_d, True)]).astype(BF16)
    onehot = jnp.asarray((np.arange(HALF_KEYS)[:, None] // L_SEL == np.arange(LANES)[None, :]), BF16)

    (za, q, kcm, vcm, ksl, vsl, kwn, vwn, kslb, vslb, kwnb, vwnb, gates, ga, gb, conv_p) = _proj_call(
        x_prompt, w_perm, cw, None, 256)
    rows = lambda t: t.reshape(B * nc, L_CMP * KVD)
    kc = _compress_prompt(rows(kcm), *cmp_k, B).reshape(B, nc, KVD)
    vc = _compress_prompt(rows(vcm), *cmp_v, B).reshape(B, nc, KVD)
    oc, msk = _cmp_call(q, kc, vc, gates, nbt)
    os_ = _sel_call(q, msk, gates, kslb, vslb, onehot, btab)
    ow = _win_call(q, gates, kwnb, vwnb, btab)
    f2 = lambda t: t.reshape(B * S, t.shape[-1])
    h = _merge_call(f2(za), f2(oc), f2(os_), f2(ow), f2(ga), f2(gb), f2(x_prompt), wuc, wua, wo, g1, b1, 256)
    y_p = _ffn_call(h, wg, wu, wd, g2, b2, 256).reshape(B, S, D_MODEL)
    kv5 = lambda t: t.reshape(1, B, S, N_KV, HEAD_DIM)
    keep = min(WINDOW, S)
    prompt_state = (kv5(kcm), kv5(vcm), kv5(ksl), kv5(vsl), kv5(kwn)[:, :, S - keep:], kv5(vwn)[:, :, S - keep:],
                    conv_p[None])

    xs = x_sample.reshape(1, DB, D_MODEL)
    p0 = state_conv[0, :, 0, :][None]
    p1 = state_conv[0, :, 1, :][None]
    (za_s, q_s, kcm_s, vcm_s, ksl_s, vsl_s, kwn_s, vwn_s, gates_s, ga_s, gb_s, u_s) = _proj_call(
        xs, w_perm, cw, (p0, p1), DB)
    pool_rows = lambda c: c[0].reshape(n_pool, PAGE_SIZE // L_CMP, L_CMP * KVD)
    kc_s = _compress_pages(page_table, pool_rows(cache_k_cmp), *cmp_k).reshape(DB, P // L_CMP, KVD)
    vc_s = _compress_pages(page_table, pool_rows(cache_v_cmp), *cmp_v).reshape(DB, P // L_CMP, KVD)
    lane_lo = jnp.arange(LANES) < HEAD_DIM
    q3 = q_s.reshape(DB, GROUP, LANES)
    qh = jnp.concatenate([jnp.where(lane_lo, q3, 0), jnp.where(lane_lo, 0, q3)], axis=1)
    gsm = gates_s.reshape(DB, LANES)[:, :3 * N_HEADS].reshape(DB, 3, GROUP, N_KV)
    gsm = gsm.transpose(1, 0, 3, 2).reshape(3, DB, N_HEADS, 1)
    row_lo = (jnp.arange(N_HEADS) < GROUP)[:, None]
    lane_own = jnp.where(row_lo, lane_lo[None, :], ~lane_lo[None, :])
    gate3 = jnp.where(lane_own, gsm, 0.0)
    ncs = P // L_CMP
    cnat = np.concatenate([np.arange(0, ncs, 2), np.arange(1, ncs, 2)])
    bc = _bias_from_dist(bias_p, P - (L_CMP * cnat + L_CMP - 1), False)
    oc_s, imp_s = _scmp_call(qh, kc_s, vc_s, gate3[0], bc)
    nsel = N_SEL - 1
    idx = _stopk_call(imp_s.reshape(DB * N_KV, ncs // 2), nsel)[:, :nsel]
    hg = lambda t: t.reshape((N_KV, GROUP) + t.shape[1:])
    zpad = jnp.zeros((N_HEADS, L_SEL), F32)
    t_last = _bias_from_dist(bias_p, L_SEL - np.arange(L_SEL), False)
    t_prev = _bias_from_dist(bias_p, 2 * L_SEL - np.arange(L_SEL), False)
    tn = jnp.stack([hg(jnp.concatenate([zpad, t_last], 1)), hg(jnp.concatenate([t_prev, zpad], 1))])
    tw = hg(_bias_from_dist(bias_p, wb - np.arange(wb), False))
    b0 = hg(bias_p[0][:, None])
    qg = q_s.reshape(DB, GROUP, N_KV, HEAD_DIM).transpose(0, 2, 1, 3)
    gate_g = gsm.reshape(3, DB, N_KV, GROUP, 1)
    rowv = lambda t: t.reshape(DB, N_KV, 1, HEAD_DIM)
    colv = lambda t: t.reshape(DB, N_KV, HEAD_DIM, 1)
    tpose = lambda t: jnp.transpose(t[0], (0, 2, 3, 1))
    os_s, ow_s, kw_out, vw_out = _ssel_call(
        idx, page_table, qg, rowv(ksl_s), rowv(vsl_s), rowv(kwn_s), rowv(vwn_s), colv(kwn_s), colv(vwn_s),
        gate_g[1], gate_g[2], tpose(state_k_win), tpose(state_v_win), tn, tw, b0,
        tpose(cache_k_slc), tpose(cache_v_slc))
    fl = lambda t: t.reshape(DB, -1)
    slots = lambda t: t.transpose(0, 2, 1, 3).reshape(DB, QD)
    h_s = _merge_call(fl(za_s), fl(oc_s), slots(os_s), slots(ow_s), fl(ga_s), fl(gb_s), fl(xs), wuc, wua, wo,
                      g1, b1, DB)
    y_s = _ffn_call(h_s, wg, wu, wd, g2, b2, DB).reshape(DB, 1, D_MODEL)
    kv5s = lambda t: t.reshape(1, DB, 1, N_KV, HEAD_DIM)
    conv_s = jnp.stack([state_conv[0, :, 1, :], u_s.reshape(DB, D_CONV)], axis=1)[None]
    untp = lambda t: jnp.transpose(t, (0, 3, 1, 2))[None]
    sample_state = (kv5s(kcm_s), kv5s(vcm_s), kv5s(ksl_s), kv5s(vsl_s), untp(kw_out), untp(vw_out), conv_s)
    return (y_p, y_s) + prompt_state + sample_state
```

```python
import functools
import math

import numpy as np
import jax
import jax.numpy as jnp
from jax import lax
from jax.experimental import pallas as pl
from jax.experimental.pallas import tpu as pltpu

F32 = jnp.float32
BF16 = jnp.bfloat16

D_MODEL = 1024
D_CONV = 512
CONV_W = 3
N_HEADS = 16
HEAD_DIM = 64
N_KV = 2
GROUP = N_HEADS // N_KV
QD = N_HEADS * HEAD_DIM
KVD = N_KV * HEAD_DIM
L_CMP = 32
L_SEL = 64
N_SEL = 16
WINDOW = 512
CMP_HID = 256
FORCED_SCORE = 1e9
N_BUCKETS = 32
MAX_DIST = 128
D_FF = -(-8 * D_MODEL // (3 * 256)) * 256
DEPTH = 1
ALPHA = (2 * DEPTH) ** 0.25
LN_EPS = 1e-5
NEG = -1e30
PAGE_SIZE = 128

LANES = 128
VMEM_LIMIT = 56 * 1024 * 1024

C_CB, C_CC, C_CH, C_Q, C_KV, C_NG, C_MG, C_END = 0, 512, 1024, 1536, 2560, 3328, 3456, 5504

TQ = 256
KT = 256
HALF_KEYS = 8192
FAR_TILES = 4


def _dot(a, b):
    return jnp.dot(a, b, preferred_element_type=F32)


def _dot_nt(a, b):
    return lax.dot_general(a, b, (((1,), (1,)), ((), ())), preferred_element_type=F32)


def _cparams(sem):
    return pltpu.CompilerParams(dimension_semantics=sem, vmem_limit_bytes=VMEM_LIMIT)


def _layer_norm(x, g, b):
    mu = jnp.mean(x, axis=-1, keepdims=True)
    xc = x - mu
    var = jnp.mean(xc * xc, axis=-1, keepdims=True)
    return xc * lax.rsqrt(var + LN_EPS) * g + b


def _bucket_table():
    d = np.arange(0, MAX_DIST + 1)
    exact = N_BUCKETS // 2
    log_ratio = (np.log(np.maximum(d, 1).astype(np.float32) / np.float32(exact))
                 / np.float32(math.log(MAX_DIST / exact))).astype(np.float32)
    large = np.minimum(exact + (log_ratio * np.float32(N_BUCKETS - exact)).astype(np.int32), N_BUCKETS - 1)
    return np.where(d < exact, d, large).astype(np.int32)


def _bias_from_dist(bias_p, dist, causal):
    dc = np.clip(dist, 0, MAX_DIST)
    val = jnp.moveaxis(bias_p[dc], -1, 0)
    if causal:
        val = jnp.where(jnp.asarray(dist < 0), NEG, val)
    return val


def _toeplitz(bias_p, off, causal):
    n = TQ
    w = _bias_from_dist(bias_p, n - 1 - np.arange(2 * n - 1) + off, causal)
    w = jnp.pad(w, ((0, 0), (0, 1)))
    rows = jnp.tile(w, (1, n))[:, :n * (2 * n - 1)].reshape(w.shape[0], n, 2 * n - 1)
    return rows[:, :, n - 1:]


def _proj_kernel(*refs, tm, sample):
    if sample:
        (x_ref, w_ref, cw_ref, p0_ref, p1_ref, za_ref, q_ref, kcm_ref, vcm_ref, ksl_ref, vsl_ref, kwn_ref,
         vwn_ref, gates_ref, ga_ref, gb_ref, u_ref) = refs
    else:
        (x_ref, w_ref, cw_ref, za_ref, q_ref, kcm_ref, vcm_ref, ksl_ref, vsl_ref, kwn_ref, vwn_ref,
         kslb_ref, vslb_ref, kwnb_ref, vwnb_ref, gates_ref, ga_ref, gb_ref, conv_ref, ubuf) = refs
    xb = x_ref[...].astype(BF16)

    def mm(c0, c1):
        return _dot(xb, w_ref[:, c0:c1])

    u = mm(C_CB, C_CC) * mm(C_CH, C_Q)
    cc = mm(C_CC, C_CH)
    cw = cw_ref[...]
    if sample:
        u1 = p1_ref[...]
        u2 = p0_ref[...]
        u_ref[...] = u
    else:
        @pl.when(pl.program_id(1) == 0)
        def _():
            ubuf[0:8, :] = jnp.zeros((8, D_CONV), F32)
        ubuf[8:tm + 8, :] = u
        u1 = ubuf[7:tm + 7, :]
        u2 = ubuf[6:tm + 6, :]
        conv_ref[...] = ubuf[tm + 6:tm + 8, :]
        ubuf[0:8, :] = ubuf[tm:tm + 8, :]
    za_ref[...] = (cc * (cw[0:1] * u2 + cw[1:2] * u1 + cw[2:3] * u)).astype(BF16)
    q_ref[...] = mm(C_Q, C_KV).astype(BF16)
    kv = mm(C_KV, C_NG)
    for i, r in enumerate((kcm_ref, vcm_ref, ksl_ref, vsl_ref, kwn_ref, vwn_ref)):
        r[...] = kv[:, i * KVD:(i + 1) * KVD]
    if not sample:
        for i, r in enumerate((kslb_ref, vslb_ref, kwnb_ref, vwnb_ref)):
            r[...] = kv[:, (i + 2) * KVD:(i + 3) * KVD].astype(BF16)
    gates_ref[...] = jax.nn.sigmoid(mm(C_NG, C_MG))
    mg = jax.nn.sigmoid(mm(C_MG, C_END))
    ga_ref[...] = mg[:, :D_MODEL]
    gb_ref[...] = mg[:, D_MODEL:]


def _proj_call(x, w_perm, conv_w, prefix, tm):
    B, T, _ = x.shape
    sample = prefix is not None
    nt = T // tm
    row = lambda w: pl.BlockSpec((None, tm, w), lambda b, s: (b, s, 0))
    full = lambda a: pl.BlockSpec(a.shape, lambda b, s: (0,) * a.ndim)
    sds = lambda w, dt: jax.ShapeDtypeStruct((B, T, w), dt)
    in_specs = [row(D_MODEL), full(w_perm), full(conv_w)]
    args = [x, w_perm, conv_w]
    out_shape = [sds(D_CONV, BF16), sds(QD, BF16)] + [sds(KVD, F32)] * 6
    out_specs = [row(D_CONV), row(QD)] + [row(KVD)] * 6
    scratch = []
    if sample:
        in_specs += [row(D_CONV), row(D_CONV)]
        args += list(prefix)
    else:
        out_shape += [sds(KVD, BF16)] * 4
        out_specs += [row(KVD)] * 4
    out_shape += [sds(LANES, F32), sds(D_MODEL, F32), sds(D_MODEL, F32)]
    out_specs += [row(LANES), row(D_MODEL), row(D_MODEL)]
    if sample:
        out_shape.append(sds(D_CONV, F32))
        out_specs.append(row(D_CONV))
    else:
        out_shape.append(jax.ShapeDtypeStruct((B, CONV_W - 1, D_CONV), F32))
        out_specs.append(pl.BlockSpec((None, CONV_W - 1, D_CONV), lambda b, s: (b, 0, 0)))
        scratch.append(pltpu.VMEM((tm + 8, D_CONV), F32))
    return pl.pallas_call(
        functools.partial(_proj_kernel, tm=tm, sample=sample), name="proj",
        grid=(B, nt), in_specs=in_specs, out_specs=out_specs, out_shape=out_shape,
        scratch_shapes=scratch, compiler_params=_cparams(("arbitrary", "arbitrary")),
    )(*args)


CMP_ROWS = 256 * L_CMP


def _compress_rows(x_ref, pos_ref, w1_ref, w2_ref, tmp_ref):
    acc = None
    for lp in range(L_CMP // 2):
        xa = x_ref[pl.ds(2 * lp, 256, stride=L_CMP), :] + pos_ref[2 * lp]
        xb = x_ref[pl.ds(2 * lp + 1, 256, stride=L_CMP), :] + pos_ref[2 * lp + 1]
        t = _dot(jnp.concatenate([xa, xb], axis=1).astype(BF16), w1_ref[lp])
        acc = t if acc is None else acc + t
    hid = jax.nn.gelu(acc)
    tmp_ref[...] = _dot(hid.astype(BF16), w2_ref[...])
    return tmp_ref[pl.ds(0, 128, stride=2), :], tmp_ref[pl.ds(1, 128, stride=2), :]


def _compress_kernel(x_ref, pos_ref, w1_ref, w2_ref, o_ref, tmp_ref):
    ev, od = _compress_rows(x_ref, pos_ref, w1_ref, w2_ref, tmp_ref)
    o_ref[0] = ev.astype(BF16)
    o_ref[1] = od.astype(BF16)


def _compress_prompt(rows, pos2, w1p, w2p):
    B, S, _ = rows.shape
    nchunk = S // CMP_ROWS
    full = lambda a: pl.BlockSpec(a.shape, lambda b, i: (0,) * a.ndim)
    return pl.pallas_call(
        _compress_kernel, name="compress", grid=(B, nchunk),
        in_specs=[pl.BlockSpec((None, CMP_ROWS, KVD), lambda b, i: (b, i, 0)), full(pos2), full(w1p), full(w2p)],
        out_specs=pl.BlockSpec((None, 2, 128, KVD), lambda b, i: (b, 0, i, 0)),
        out_shape=jax.ShapeDtypeStruct((B, 2, nchunk * 128, KVD), BF16),
        scratch_shapes=[pltpu.VMEM((256, KVD), F32)],
        compiler_params=_cparams(("arbitrary", "arbitrary")),
    )(rows, pos2, w1p, w2p)


def _compress_pages_kernel(pt_ref, pool_ref, pos_ref, w1_ref, w2_ref, o_ref, buf, nat, tmp_ref, sem, *, n_pages):
    b = pl.program_id(0)
    nb = pl.num_programs(0)

    def copy(bb, p, slot):
        page = pt_ref[bb * n_pages + p]
        return pltpu.make_async_copy(pool_ref.at[page], buf.at[slot, pl.ds(PAGE_SIZE * p, PAGE_SIZE), :],
                                     sem.at[slot])

    def start_all(bb, slot):
        for p in range(n_pages):
            copy(bb, p, slot).start()

    slot = lax.rem(b, 2)

    @pl.when(b == 0)
    def _():
        start_all(b, 0)

    for p in range(n_pages):
        copy(b, p, slot).wait()

    @pl.when(b + 1 < nb)
    def _():
        start_all(b + 1, 1 - slot)

    for p in range(n_pages):
        rows = pl.ds(PAGE_SIZE * p, PAGE_SIZE)
        nat[rows, :] = buf[slot, rows, :].T
    ev, od = _compress_rows(nat, pos_ref, w1_ref, w2_ref, tmp_ref)
    o_ref[0] = ev.astype(BF16)
    o_ref[1] = od.astype(BF16)


def _compress_pages(page_table, pool_t, pos2, w1p, w2p):
    DB, n_pages = page_table.shape
    assert n_pages * PAGE_SIZE == CMP_ROWS and KVD == PAGE_SIZE
    full = lambda a: pl.BlockSpec(a.shape, lambda b, pt: (0,) * a.ndim)
    return pl.pallas_call(
        functools.partial(_compress_pages_kernel, n_pages=n_pages), name="compress_pages",
        grid_spec=pltpu.PrefetchScalarGridSpec(
            num_scalar_prefetch=1, grid=(DB,),
            in_specs=[pl.BlockSpec(memory_space=pl.ANY), full(pos2), full(w1p), full(w2p)],
            out_specs=pl.BlockSpec((None, 2, 128, KVD), lambda b, pt: (b, 0, 0, 0)),
            scratch_shapes=[pltpu.VMEM((2, CMP_ROWS, KVD), F32), pltpu.VMEM((CMP_ROWS, KVD), F32),
                            pltpu.VMEM((256, KVD), F32), pltpu.SemaphoreType.DMA((2,))]),
        out_shape=jax.ShapeDtypeStruct((DB, 2, 128, KVD), BF16),
        compiler_params=_cparams(("arbitrary",)),
    )(page_table.reshape(-1), pool_t, pos2, w1p, w2p)


def _topk_extract(v, k, want_idx):
    lanef = lax.broadcasted_iota(jnp.int32, v.shape, 1).astype(F32)
    idx = []
    for _ in range(k):
        mx = jnp.max(v, axis=-1, keepdims=True)
        first = jnp.min(jnp.where(v == mx, lanef, 1e9), axis=-1, keepdims=True)
        v = jnp.where(lanef == first, -jnp.inf, v)
        if want_idx:
            idx.append(first)
    return v, idx


def _cmp_kernel(q_ref, kc_ref, vc_ref, gates_ref, nb_ref, oc_ref, m_ref, *, nc):
    t0 = pl.program_id(1) * TQ
    half = nc // 2
    lane = lax.broadcasted_iota(jnp.int32, (nc, LANES), 1)
    crow = lax.broadcasted_iota(jnp.int32, (nc, LANES), 0)
    cnat = jnp.where(crow < half, 2 * crow, 2 * (crow - half) + 1)
    cbase = t0 // L_CMP - 8
    e = jnp.where(lane < 16, lane, lane - 16)
    ext = jnp.where((lane < 32) & (cnat == cbase + e), 1.0, 0.0)
    ext = jnp.where((lane == 32) & (cnat >= cbase + 16), NEG, ext).astype(BF16)
    lo = lane < HEAD_DIM
    kc = kc_ref[...]
    vc = vc_ref[...]
    zero = jnp.zeros_like(kc)
    rhs = [jnp.concatenate([jnp.where(lo, kc, zero), ext], axis=1),
           jnp.concatenate([jnp.where(lo, zero, kc), ext], axis=1)]
    vg = [jnp.where(lo, vc, zero), jnp.where(lo, zero, vc)]
    gates = gates_ref[...]
    imp = [jnp.zeros((TQ, nc), F32), jnp.zeros((TQ, nc), F32)]
    for j in range(GROUP):
        qs = q_ref[:, j * LANES:(j + 1) * LANES]
        osl = jnp.zeros((TQ, LANES), F32)
        for g in range(N_KV):
            hh = 2 * j + g
            lhs = jnp.concatenate([qs, nb_ref[hh]], axis=1)
            s = _dot_nt(lhs, rhs[g])
            m = jnp.max(s, axis=-1, keepdims=True)
            ex = jnp.exp(s - m)
            l = jnp.sum(ex, axis=-1, keepdims=True)
            p = ex * jnp.where(m > 0.1 * NEG, 1.0 / l, 0.0)
            imp[g] = imp[g] + p
            osl = osl + gates[:, hh:hh + 1] * _dot(p.astype(BF16), vg[g])
        oc_ref[:, j * LANES:(j + 1) * LANES] = osl
    blk = lax.broadcasted_iota(jnp.int32, (TQ, half), 1)
    trow = t0 + lax.broadcasted_iota(jnp.int32, (TQ, half), 0)
    forced = (blk == trow // L_SEL) | (blk == 0)
    started = blk * L_SEL <= trow
    for g in range(N_KV):
        v = imp[g][:, :half] + imp[g][:, half:]
        v = jnp.where(forced, FORCED_SCORE, jnp.where(started, v, -1.0))
        v, _ = _topk_extract(v, N_SEL, False)
        m_ref[:, g * half:(g + 1) * half] = jnp.where(v == -jnp.inf, 0.0, NEG).astype(BF16)


def _cmp_call(q, kc, vc, gates, nbt):
    B, S, _ = q.shape
    nc = kc.shape[1]
    row = lambda w: pl.BlockSpec((None, TQ, w), lambda b, t: (b, t, 0))
    perb = lambda a: pl.BlockSpec((None,) + a.shape[1:], lambda b, t: (b,) + (0,) * (a.ndim - 1))
    full = lambda a: pl.BlockSpec(a.shape, lambda b, t: (0,) * a.ndim)
    return pl.pallas_call(
        functools.partial(_cmp_kernel, nc=nc), name="cmp", grid=(B, S // TQ),
        in_specs=[row(QD), perb(kc), perb(vc), row(LANES), full(nbt)],
        out_specs=[row(QD), row(nc)],
        out_shape=[jax.ShapeDtypeStruct((B, S, QD), F32), jax.ShapeDtypeStruct((B, S, nc), BF16)],
        compiler_params=_cparams(("arbitrary", "arbitrary")),
    )(q, kc, vc, gates, nbt)


def _sel_kernel(q_ref, m_ref, gates_ref, k_ref, v_ref, oh_ref, bt_ref, os_ref, lhs_s, acc_s, m_s, *, nh):
    qt = pl.program_id(1)
    tph = HALF_KEYS // KT
    for j in range(GROUP):
        qs = q_ref[:, j * LANES:(j + 1) * LANES]
        for h in range(nh):
            for g in range(N_KV):
                c0 = (g * nh + h) * LANES
                lhs_s[(h * 2 + g) * GROUP + j] = jnp.concatenate([qs, m_ref[:, c0:c0 + LANES]], axis=1)
    m_s[...] = jnp.full(m_s.shape, NEG, F32)
    acc_s[...] = jnp.zeros(acc_s.shape, F32)
    lo_q = lax.broadcasted_iota(jnp.int32, (TQ, LANES), 1) < HEAD_DIM

    def step(kt, nk, kind):
        rows = nk * KT
        k_t = k_ref[pl.ds(pl.multiple_of(kt * KT, KT), rows), :]
        v_t = v_ref[pl.ds(pl.multiple_of(kt * KT, KT), rows), :]
        h = kt // tph
        oh = oh_ref[pl.ds(pl.multiple_of((kt - h * tph) * KT, KT), rows), :]
        lane = lax.broadcasted_iota(jnp.int32, (rows, LANES), 1)
        lo_k = lane < HEAD_DIM
        zero = jnp.zeros_like(k_t)
        rhs = [jnp.concatenate([jnp.where(lo_k, k_t, zero), oh], axis=1),
               jnp.concatenate([jnp.where(lo_k, zero, k_t), oh], axis=1)]
        vg = [jnp.where(lo_k, v_t, jnp.where(lane == HEAD_DIM, 1.0, 0.0).astype(BF16)),
              jnp.where(lo_k, jnp.where(lane == 0, 1.0, 0.0).astype(BF16), v_t)]
        for j in range(GROUP):
            for g in range(N_KV):
                hh = 2 * j + g
                s = _dot_nt(lhs_s[(h * 2 + g) * GROUP + j], rhs[g])
                if kind:
                    s = s + bt_ref[kind - 1, hh].astype(F32)
                mp = m_s[hh]
                mc = jnp.broadcast_to(jnp.max(s, axis=-1, keepdims=True), (TQ, LANES))
                mn = jnp.maximum(mp, mc)
                m_s[hh] = mn
                p = jnp.exp(s - jnp.concatenate([mn] * (rows // LANES), axis=1))
                acc_s[hh] = acc_s[hh] * jnp.exp(mp - mn) + _dot(p.astype(BF16), vg[g])

    step(qt, 1, 2)

    @pl.when(qt >= 1)
    def _():
        step(qt - 1, 1, 1)

    nfar = jnp.maximum(qt - 1, 0)
    nbig = nfar // FAR_TILES

    def big(i, c):
        step(i * FAR_TILES, FAR_TILES, False)
        return c

    def small(kt, c):
        step(kt, 1, False)
        return c

    lax.fori_loop(0, nbig, big, 0)
    lax.fori_loop(nbig * FAR_TILES, nfar, small, 0)

    gates = gates_ref[...]
    for j in range(GROUP):
        a0 = acc_s[2 * j]
        a1 = acc_s[2 * j + 1]
        c = N_HEADS + 2 * j
        sc0 = gates[:, c:c + 1] / a0[:, HEAD_DIM:HEAD_DIM + 1]
        sc1 = gates[:, c + 1:c + 2] / a1[:, 0:1]
        os_ref[:, j * LANES:(j + 1) * LANES] = jnp.where(lo_q, a0 * sc0, a1 * sc1)


def _sel_call(q, msk, gates, kb, vb, onehot, btab):
    B, S, _ = q.shape
    nh = msk.shape[2] // (2 * LANES)
    row = lambda w: pl.BlockSpec((None, TQ, w), lambda b, t: (b, t, 0))
    perb = lambda a: pl.BlockSpec((None,) + a.shape[1:], lambda b, t: (b,) + (0,) * (a.ndim - 1))
    full = lambda a: pl.BlockSpec(a.shape, lambda b, t: (0,) * a.ndim)
    return pl.pallas_call(
        functools.partial(_sel_kernel, nh=nh), name="sel", grid=(B, S // TQ),
        in_specs=[row(QD), row(msk.shape[2]), row(LANES), perb(kb), perb(vb), full(onehot), full(btab)],
        out_specs=row(QD),
        out_shape=jax.ShapeDtypeStruct((B, S, QD), F32),
        scratch_shapes=[pltpu.VMEM((nh * 2 * GROUP, TQ, 2 * LANES), BF16), pltpu.VMEM((N_HEADS, TQ, LANES), F32),
                        pltpu.VMEM((N_HEADS, TQ, LANES), F32)],
        compiler_params=_cparams(("arbitrary", "arbitrary")),
    )(q, msk, gates, kb, vb, onehot, btab)


def _win_kernel(q_ref, gates_ref, k0_ref, k1_ref, k2_ref, v0_ref, v1_ref, v2_ref, bt_ref, ow_ref):
    qt = pl.program_id(1)
    lo_k = lax.broadcasted_iota(jnp.int32, (3 * KT, LANES), 1) < HEAD_DIM
    kcat = jnp.concatenate([k0_ref[...], k1_ref[...], k2_ref[...]], axis=0)
    vcat = jnp.concatenate([v0_ref[...], v1_ref[...], v2_ref[...]], axis=0)
    zero = jnp.zeros_like(kcat)
    kg = [jnp.where(lo_k, kcat, zero), jnp.where(lo_k, zero, kcat)]
    vg = [jnp.where(lo_k, vcat, zero), jnp.where(lo_k, zero, vcat)]
    ii = lax.broadcasted_iota(jnp.int32, (TQ, KT), 0)
    jj = lax.broadcasted_iota(jnp.int32, (TQ, KT), 1)
    pen0 = jnp.where((jj >= ii) & (qt >= 2), 0.0, NEG)
    pen1 = jnp.where(qt >= 1, 0.0, NEG)
    gates = gates_ref[...]
    for j in range(GROUP):
        qs = q_ref[:, j * LANES:(j + 1) * LANES]
        osl = jnp.zeros((TQ, LANES), F32)
        for g in range(N_KV):
            hh = 2 * j + g
            s = _dot_nt(qs, kg[g])
            s = s + jnp.concatenate([pen0, bt_ref[0, hh].astype(F32) + pen1, bt_ref[1, hh].astype(F32)], axis=1)
            m = jnp.max(s, axis=-1, keepdims=True)
            ex = jnp.exp(s - m)
            l = jnp.sum(ex, axis=-1, keepdims=True)
            c = 2 * N_HEADS + hh
            osl = osl + (gates[:, c:c + 1] / l) * _dot(ex.astype(BF16), vg[g])
        ow_ref[:, j * LANES:(j + 1) * LANES] = osl


def _win_call(q, gates, kb, vb, btab):
    B, S, _ = q.shape
    row = lambda w: pl.BlockSpec((None, TQ, w), lambda b, t: (b, t, 0))
    back = lambda d: pl.BlockSpec((None, KT, KVD), lambda b, t: (b, jnp.maximum(t - d, 0), 0))
    full = lambda a: pl.BlockSpec(a.shape, lambda b, t: (0,) * a.ndim)
    return pl.pallas_call(
        _win_kernel, name="win", grid=(B, S // TQ),
        in_specs=[row(QD), row(LANES), back(2), back(1), back(0), back(2), back(1), back(0), full(btab)],
        out_specs=row(QD),
        out_shape=jax.ShapeDtypeStruct((B, S, QD), F32),
        compiler_params=_cparams(("arbitrary", "arbitrary")),
    )(q, gates, kb, kb, kb, vb, vb, vb, btab)


def _merge_kernel(za_ref, oc_ref, os_ref, ow_ref, ga_ref, gb_ref, x_ref, wuc_ref, wua_ref, wo_ref, g_ref, b_ref,
                  h_ref):
    a = _dot(za_ref[...], wuc_ref[...])
    ob = (oc_ref[...] + os_ref[...] + ow_ref[...]).astype(BF16)
    bm = _dot(ob, wua_ref[...])
    pre = ga_ref[...] * a + gb_ref[...] * bm
    mix = _dot(pre.astype(BF16), wo_ref[...])
    h_ref[...] = _layer_norm(ALPHA * x_ref[...] + mix, g_ref[...], b_ref[...])


def _merge_call(za, oc, os_, ow, ga, gb, x, wuc, wua, wo, g, b, tm):
    R = x.shape[0]
    row = lambda w: pl.BlockSpec((tm, w), lambda i: (i, 0))
    full = lambda a: pl.BlockSpec(a.shape, lambda i: (0,) * a.ndim)
    return pl.pallas_call(
        _merge_kernel, name="merge", grid=(R // tm,),
        in_specs=[row(D_CONV), row(QD), row(QD), row(QD), row(D_MODEL), row(D_MODEL), row(D_MODEL),
                  full(wuc), full(wua), full(wo), full(g), full(b)],
        out_specs=row(D_MODEL), out_shape=jax.ShapeDtypeStruct((R, D_MODEL), F32),
        compiler_params=_cparams(("arbitrary",)),
    )(za, oc, os_, ow, ga, gb, x, wuc, wua, wo, g, b)


def _ffn_kernel(h_ref, wg_ref, wu_ref, wd_ref, g_ref, b_ref, y_ref):
    h = h_ref[...]
    hb = h.astype(BF16)
    act = jax.nn.silu(_dot(hb, wg_ref[...])) * _dot(hb, wu_ref[...])
    f = _dot(act.astype(BF16), wd_ref[...])
    y_ref[...] = _layer_norm(ALPHA * h + f, g_ref[...], b_ref[...])


def _ffn_call(h, wg, wu, wd, g, b, tm):
    R = h.shape[0]
    row = lambda w: pl.BlockSpec((tm, w), lambda i: (i, 0))
    full = lambda a: pl.BlockSpec(a.shape, lambda i: (0,) * a.ndim)
    return pl.pallas_call(
        _ffn_kernel, name="ffn", grid=(R // tm,),
        in_specs=[row(D_MODEL), full(wg), full(wu), full(wd), full(g), full(b)],
        out_specs=row(D_MODEL), out_shape=jax.ShapeDtypeStruct((R, D_MODEL), F32),
        compiler_params=_cparams(("arbitrary",)),
    )(h, wg, wu, wd, g, b)


def _scmp_kernel(qh_ref, kc_ref, vc_ref, gate_ref, bc_ref, oc_ref, imp_ref, *, nc):
    half = nc // 2
    s = _dot_nt(qh_ref[...], kc_ref[...]) + bc_ref[...]
    m = jnp.max(s, axis=-1, keepdims=True)
    ex = jnp.exp(s - m)
    p = ex * (1.0 / jnp.sum(ex, axis=-1, keepdims=True))
    o = _dot(p.astype(BF16), vc_ref[...]) * gate_ref[...]
    oc_ref[...] = o[0:GROUP] + o[GROUP:2 * GROUP]
    for g in range(N_KV):
        pg = jnp.sum(p[g * GROUP:(g + 1) * GROUP], axis=0, keepdims=True)
        imp_ref[g:g + 1, :] = pg[:, :half] + pg[:, half:]


def _scmp_call(qh, kc, vc, gate_c, bc):
    DB, _, nc, _ = (kc.shape[0], None, kc.shape[1], None)
    perb = lambda a: pl.BlockSpec((None,) + a.shape[1:], lambda b: (b,) + (0,) * (a.ndim - 1))
    full = lambda a: pl.BlockSpec(a.shape, lambda b: (0,) * a.ndim)
    return pl.pallas_call(
        functools.partial(_scmp_kernel, nc=nc), name="sample_cmp", grid=(DB,),
        in_specs=[perb(qh), perb(kc), perb(vc), perb(gate_c), full(bc)],
        out_specs=[pl.BlockSpec((None, GROUP, LANES), lambda b: (b, 0, 0)),
                   pl.BlockSpec((None, N_KV, nc // 2), lambda b: (b, 0, 0))],
        out_shape=[jax.ShapeDtypeStruct((DB, GROUP, LANES), F32), jax.ShapeDtypeStruct((DB, N_KV, nc // 2), F32)],
        compiler_params=_cparams(("arbitrary",)),
    )(qh, kc, vc, gate_c, bc)


def _stopk_kernel(imp_ref, idx_ref, *, k):
    v = imp_ref[...]
    lane = lax.broadcasted_iota(jnp.int32, v.shape, 1)
    v = jnp.where(lane == 0, FORCED_SCORE, v)
    _, idx = _topk_extract(v, k, True)
    out = jnp.zeros(v.shape, F32)
    for i, col in enumerate(idx):
        out = jnp.where(lane == i, col, out)
    idx_ref[...] = out.astype(jnp.int32)


def _stopk_call(imp, k):
    return pl.pallas_call(
        functools.partial(_stopk_kernel, k=k), name="sample_topk",
        out_shape=jax.ShapeDtypeStruct(imp.shape, jnp.int32),
        compiler_params=pltpu.CompilerParams(vmem_limit_bytes=VMEM_LIMIT),
    )(imp)


def _ssel_kernel(idx_ref, pt_ref, q_ref, knew_ref, vnew_ref, kwnew_ref, vwnew_ref, kwcol_ref, vwcol_ref,
                 gate_s_ref, gate_w_ref, kwin_ref, vwin_ref, tn_ref, tw_ref, b0_ref, kpool_ref, vpool_ref,
                 os_ref, ow_ref, kwo_ref, vwo_ref, kbuf, vbuf, sem, *, n_pages, nsel, n_last):
    b = pl.program_id(0)

    def copies(g, s):
        n = idx_ref[(b * N_KV + g) * nsel + s]
        page = pt_ref[b * n_pages + n // 2]
        dst = pl.ds(s * PAGE_SIZE, PAGE_SIZE)
        return (pltpu.make_async_copy(kpool_ref.at[page, g], kbuf.at[g, :, dst], sem.at[0]),
                pltpu.make_async_copy(vpool_ref.at[page, g], vbuf.at[g, :, dst], sem.at[1]))

    for g in range(N_KV):
        for s in range(nsel):
            ck, cv = copies(g, s)
            ck.start()
            cv.start()

    def attend(qg, s, k_new, v_new, vt, bias_new, gate):
        s_new = jnp.sum(qg.astype(F32) * k_new.astype(BF16).astype(F32), axis=-1, keepdims=True) + bias_new
        m = jnp.maximum(jnp.max(s, axis=-1, keepdims=True), s_new)
        ex = jnp.exp(s - m)
        ex_new = jnp.exp(s_new - m)
        l = jnp.sum(ex, axis=-1, keepdims=True) + ex_new
        o = _dot_nt(ex.astype(BF16), vt.astype(BF16)) + ex_new.astype(BF16).astype(F32) * v_new.astype(BF16).astype(F32)
        return o * (gate / l)

    wb = kwin_ref.shape[-1]
    lane_w = lax.broadcasted_iota(jnp.int32, (HEAD_DIM, wb), 1)
    for g in range(N_KV):
        qg = q_ref[g]
        kw = kwin_ref[g]
        vw = vwin_ref[g]
        s = _dot(qg, kw.astype(BF16)) + tw_ref[g]
        ow_ref[g] = attend(qg, s, kwnew_ref[g], vwnew_ref[g], vw, b0_ref[g], gate_w_ref[g])
        kwo_ref[g] = jnp.where(lane_w == wb - 1, kwcol_ref[g], pltpu.roll(kw, wb - 1, 1))
        vwo_ref[g] = jnp.where(lane_w == wb - 1, vwcol_ref[g], pltpu.roll(vw, wb - 1, 1))

    for g in range(N_KV):
        for s in range(nsel):
            ck, cv = copies(g, s)
            ck.wait()
            cv.wait()

    upper = lax.broadcasted_iota(jnp.int32, (GROUP, PAGE_SIZE), 1) >= L_SEL
    keep_upper = jnp.where(upper, 0.0, NEG)
    keep_lower = jnp.where(upper, NEG, 0.0)
    for g in range(N_KV):
        qg = q_ref[g]
        pieces = []
        for s in range(nsel):
            n = idx_ref[(b * N_KV + g) * nsel + s]
            piece = jnp.where(n % 2 == 1, keep_upper, keep_lower)
            piece = piece + jnp.where(n == n_last, 1.0, 0.0) * tn_ref[0, g]
            piece = piece + jnp.where(n == n_last - 1, 1.0, 0.0) * tn_ref[1, g]
            pieces.append(piece)
        s = _dot(qg, kbuf[g].astype(BF16)) + jnp.concatenate(pieces, axis=1)
        os_ref[g] = attend(qg, s, knew_ref[g], vnew_ref[g], vbuf[g], b0_ref[g], gate_s_ref[g])


def _ssel_call(idx, page_table, qg, knew, vnew, kwnew, vwnew, kwcol, vwcol, gate_s, gate_w, kwin, vwin, tn, tw, b0,
               kpool, vpool):
    DB, n_pages = page_table.shape
    nsel = idx.shape[-1]
    wb = kwin.shape[-1]
    n_last = n_pages * (PAGE_SIZE // L_SEL) - 1
    perb = lambda a: pl.BlockSpec((None,) + a.shape[1:], lambda b, *_: (b,) + (0,) * (a.ndim - 1))
    full = lambda a: pl.BlockSpec(a.shape, lambda b, *_: (0,) * a.ndim)
    anyspec = pl.BlockSpec(memory_space=pl.ANY)
    head_o = pl.BlockSpec((None, N_KV, GROUP, HEAD_DIM), lambda b, *_: (b, 0, 0, 0))
    win_o = pl.BlockSpec((None, N_KV, HEAD_DIM, wb), lambda b, *_: (b, 0, 0, 0))
    head_s = jax.ShapeDtypeStruct((DB, N_KV, GROUP, HEAD_DIM), F32)
    win_s = jax.ShapeDtypeStruct((DB, N_KV, HEAD_DIM, wb), F32)
    return pl.pallas_call(
        functools.partial(_ssel_kernel, n_pages=n_pages, nsel=nsel, n_last=n_last), name="sample_sel_win",
        grid_spec=pltpu.PrefetchScalarGridSpec(
            num_scalar_prefetch=2, grid=(DB,),
            in_specs=[perb(qg), perb(knew), perb(vnew), perb(kwnew), perb(vwnew), perb(kwcol), perb(vwcol),
                      perb(gate_s), perb(gate_w), perb(kwin), perb(vwin), full(tn), full(tw), full(b0),
                      anyspec, anyspec],
            out_specs=[head_o, head_o, win_o, win_o],
            scratch_shapes=[pltpu.VMEM((N_KV, HEAD_DIM, nsel * PAGE_SIZE), F32),
                            pltpu.VMEM((N_KV, HEAD_DIM, nsel * PAGE_SIZE), F32),
                            pltpu.SemaphoreType.DMA((2,))]),
        out_shape=[head_s, head_s, win_s, win_s],
        compiler_params=_cparams(("arbitrary",)),
    )(idx.reshape(-1), page_table.reshape(-1), qg, knew, vnew, kwnew, vwnew, kwcol, vwcol, gate_s, gate_w,
      kwin, vwin, tn, tw, b0, kpool, vpool)


def _perm_w_in(w_in):
    o_q = 3 * D_CONV
    o_kv = o_q + QD
    o_ng = o_kv + 6 * KVD
    o_mg = o_ng + 3 * N_HEADS
    q = w_in[:, o_q:o_kv].reshape(D_MODEL, N_KV, GROUP, HEAD_DIM).transpose(0, 2, 1, 3).reshape(D_MODEL, QD)
    q = q * (HEAD_DIM ** -0.5)
    ng = w_in[:, o_ng:o_mg].reshape(D_MODEL, N_KV, GROUP, 3).transpose(0, 3, 2, 1).reshape(D_MODEL, 3 * N_HEADS)
    ng = jnp.pad(ng, ((0, 0), (0, LANES - 3 * N_HEADS)))
    return jnp.concatenate([w_in[:, :o_q], q, w_in[:, o_kv:o_ng], ng, w_in[:, o_mg:]], axis=1).astype(BF16)


def _perm_heads(t):
    return t.reshape(t.shape[:-1] + (N_KV, GROUP)).swapaxes(-1, -2).reshape(t.shape)


def _cmp_weights(w1, w2, pos):
    eye = jnp.eye(N_KV, dtype=F32)
    w1p = jnp.einsum('ldh,ab->ladbh', w1.reshape(L_CMP, HEAD_DIM, CMP_HID), eye)
    w1p = w1p.reshape(L_CMP // 2, 2 * KVD, N_KV * CMP_HID).astype(BF16)
    w2p = jnp.einsum('hd,ab->ahbd', w2, eye).reshape(N_KV * CMP_HID, KVD).astype(BF16)
    pos2 = jnp.tile(pos[:, None, :], (1, N_KV, 1)).reshape(L_CMP, 1, KVD)
    return pos2, w1p, w2p


def kernel(x_prompt, x_sample, cache_k_cmp, cache_v_cmp, cache_k_slc, cache_v_slc, state_k_win, state_v_win,
           state_conv, page_table, w_in, conv_w, w_cmp_k1, w_cmp_k2, pos_cmp_k, w_cmp_v1, w_cmp_v2, pos_cmp_v,
           rel_bias, w_up_conv, w_up_attn, w_o, ln1_g, ln1_b, w_gate_ffn, w_up_ffn, w_down_ffn, ln2_g, ln2_b):
    B, S, _ = x_prompt.shape
    DB = x_sample.shape[0]
    n_pages = page_table.shape[1]
    P = n_pages * PAGE_SIZE
    n_pool = cache_k_cmp.shape[1]
    wb = state_k_win.shape[2]
    assert x_sample.shape[1] == 1 and w_in.shape[0] == 1
    assert S % HALF_KEYS == 0 and P == HALF_KEYS and wb == WINDOW and S >= WINDOW
    nc = S // L_CMP
    nh = S // HALF_KEYS

    w_perm = _perm_w_in(w_in[0])
    cw = conv_w[0]
    cmp_k = _cmp_weights(w_cmp_k1[0], w_cmp_k2[0], pos_cmp_k[0])
    cmp_v = _cmp_weights(w_cmp_v1[0], w_cmp_v2[0], pos_cmp_v[0])
    wuc = w_up_conv[0].astype(BF16)
    wua = w_up_attn[0].reshape(N_KV, GROUP, HEAD_DIM, D_MODEL).transpose(1, 0, 2, 3).reshape(QD, D_MODEL).astype(BF16)
    wo = w_o[0].astype(BF16)
    wg, wu, wd = w_gate_ffn[0].astype(BF16), w_up_ffn[0].astype(BF16), w_down_ffn[0].astype(BF16)
    g1, b1, g2, b2 = ln1_g, ln1_b, ln2_g, ln2_b

    bucket = _bucket_table()
    bias_p = rel_bias[bucket] - rel_bias[N_BUCKETS - 1][None, :]
    bias_pp = _perm_heads(bias_p)
    ii = np.arange(TQ)[:, None]
    dist_c = ii - L_CMP * np.arange(16)[None, :] + 8 * L_CMP - (L_CMP - 1)
    nb = _bias_from_dist(bias_pp, dist_c, True)
    nb_hi = nb.astype(BF16)
    nb_lo = jnp.where(jnp.asarray(dist_c < 0), 0.0, nb - nb_hi.astype(F32)).astype(BF16)
    nbt = jnp.concatenate([nb_hi, nb_lo, jnp.ones((N_HEADS, TQ, 1), BF16),
                           jnp.zeros((N_HEADS, TQ, LANES - 33), BF16)], axis=-1)
    btab = jnp.stack([_toeplitz(bias_pp, KT, False), _toeplitz(bias_pp, 0, True)]).astype(BF16)
    onehot = jnp.asarray((np.arange(HALF_KEYS)[:, None] // L_SEL == np.arange(LANES)[None, :]), BF16)

    (za, q, kcm, vcm, ksl, vsl, kwn, vwn, kslb, vslb, kwnb, vwnb, gates, ga, gb, conv_p) = _proj_call(
        x_prompt, w_perm, cw, None, 256)
    kc = _compress_prompt(kcm, *cmp_k).reshape(B, nc, KVD)
    vc = _compress_prompt(vcm, *cmp_v).reshape(B, nc, KVD)
    oc, msk = _cmp_call(q, kc, vc, gates, nbt)
    os_ = _sel_call(q, msk, gates, kslb, vslb, onehot, btab)
    ow = _win_call(q, gates, kwnb, vwnb, btab)
    f2 = lambda t: t.reshape(B * S, t.shape[-1])
    h = _merge_call(f2(za), f2(oc), f2(os_), f2(ow), f2(ga), f2(gb), f2(x_prompt), wuc, wua, wo, g1, b1, 256)
    y_p = _ffn_call(h, wg, wu, wd, g2, b2, 256).reshape(B, S, D_MODEL)
    kv5 = lambda t: t.reshape(1, B, S, N_KV, HEAD_DIM)
    keep = min(WINDOW, S)
    prompt_state = (kv5(kcm), kv5(vcm), kv5(ksl), kv5(vsl), kv5(kwn)[:, :, S - keep:], kv5(vwn)[:, :, S - keep:],
                    conv_p[None])

    xs = x_sample.reshape(1, DB, D_MODEL)
    p0 = state_conv[0, :, 0, :][None]
    p1 = state_conv[0, :, 1, :][None]
    (za_s, q_s, kcm_s, vcm_s, ksl_s, vsl_s, kwn_s, vwn_s, gates_s, ga_s, gb_s, u_s) = _proj_call(
        xs, w_perm, cw, (p0, p1), DB)
    pool_rows = lambda c: jnp.transpose(c[0], (0, 2, 3, 1)).reshape(n_pool, KVD, PAGE_SIZE)
    kc_s = _compress_pages(page_table, pool_rows(cache_k_cmp), *cmp_k).reshape(DB, P // L_CMP, KVD)
    vc_s = _compress_pages(page_table, pool_rows(cache_v_cmp), *cmp_v).reshape(DB, P // L_CMP, KVD)
    lane_lo = jnp.arange(LANES) < HEAD_DIM
    q3 = q_s.reshape(DB, GROUP, LANES)
    qh = jnp.concatenate([jnp.where(lane_lo, q3, 0), jnp.where(lane_lo, 0, q3)], axis=1)
    gsm = gates_s.reshape(DB, LANES)[:, :3 * N_HEADS].reshape(DB, 3, GROUP, N_KV)
    gsm = gsm.transpose(1, 0, 3, 2).reshape(3, DB, N_HEADS, 1)
    row_lo = (jnp.arange(N_HEADS) < GROUP)[:, None]
    lane_own = jnp.where(row_lo, lane_lo[None, :], ~lane_lo[None, :])
    gate3 = jnp.where(lane_own, gsm, 0.0)
    ncs = P // L_CMP
    cnat = np.concatenate([np.arange(0, ncs, 2), np.arange(1, ncs, 2)])
    bc = _bias_from_dist(bias_p, P - (L_CMP * cnat + L_CMP - 1), False)
    oc_s, imp_s = _scmp_call(qh, kc_s, vc_s, gate3[0], bc)
    nsel = N_SEL - 1
    idx = _stopk_call(imp_s.reshape(DB * N_KV, ncs // 2), nsel)[:, :nsel]
    hg = lambda t: t.reshape((N_KV, GROUP) + t.shape[1:])
    zpad = jnp.zeros((N_HEADS, L_SEL), F32)
    t_last = _bias_from_dist(bias_p, L_SEL - np.arange(L_SEL), False)
    t_prev = _bias_from_dist(bias_p, 2 * L_SEL - np.arange(L_SEL), False)
    tn = jnp.stack([hg(jnp.concatenate([zpad, t_last], 1)), hg(jnp.concatenate([t_prev, zpad], 1))])
    tw = hg(_bias_from_dist(bias_p, wb - np.arange(wb), False))
    b0 = hg(bias_p[0][:, None])
    qg = q_s.reshape(DB, GROUP, N_KV, HEAD_DIM).transpose(0, 2, 1, 3)
    gate_g = gsm.reshape(3, DB, N_KV, GROUP, 1)
    rowv = lambda t: t.reshape(DB, N_KV, 1, HEAD_DIM)
    colv = lambda t: t.reshape(DB, N_KV, HEAD_DIM, 1)
    tpose = lambda t: jnp.transpose(t[0], (0, 2, 3, 1))
    os_s, ow_s, kw_out, vw_out = _ssel_call(
        idx, page_table, qg, rowv(ksl_s), rowv(vsl_s), rowv(kwn_s), rowv(vwn_s), colv(kwn_s), colv(vwn_s),
        gate_g[1], gate_g[2], tpose(state_k_win), tpose(state_v_win), tn, tw, b0,
        tpose(cache_k_slc), tpose(cache_v_slc))
    fl = lambda t: t.reshape(DB, -1)
    slots = lambda t: t.transpose(0, 2, 1, 3).reshape(DB, QD)
    h_s = _merge_call(fl(za_s), fl(oc_s), slots(os_s), slots(ow_s), fl(ga_s), fl(gb_s), fl(xs), wuc, wua, wo,
                      g1, b1, DB)
    y_s = _ffn_call(h_s, wg, wu, wd, g2, b2, DB).reshape(DB, 1, D_MODEL)
    kv5s = lambda t: t.reshape(1, DB, 1, N_KV, HEAD_DIM)
    conv_s = jnp.stack([state_conv[0, :, 1, :], u_s.reshape(DB, D_CONV)], axis=1)[None]
    untp = lambda t: jnp.transpose(t, (0, 3, 1, 2))[None]
    sample_state = (kv5s(kcm_s), kv5s(vcm_s), kv5s(ksl_s), kv5s(vsl_s), untp(kw_out), untp(vw_out), conv_s)
    return (y_p, y_s) + prompt_state + sample_state
```

```python
import functools
import math

import numpy as np
import jax
import jax.numpy as jnp
from jax import lax
from jax.experimental import pallas as pl
from jax.experimental.pallas import tpu as pltpu

F32 = jnp.float32
BF16 = jnp.bfloat16

D_MODEL = 1024
D_CONV = 512
CONV_W = 3
N_HEADS = 16
HEAD_DIM = 64
N_KV = 2
GROUP = N_HEADS // N_KV
QD = N_HEADS * HEAD_DIM
KVD = N_KV * HEAD_DIM
L_CMP = 32
L_SEL = 64
N_SEL = 16
WINDOW = 512
CMP_HID = 256
FORCED_SCORE = 1e9
N_BUCKETS = 32
MAX_DIST = 128
D_FF = -(-8 * D_MODEL // (3 * 256)) * 256
DEPTH = 1
ALPHA = (2 * DEPTH) ** 0.25
LN_EPS = 1e-5
NEG = -1e30
PAGE_SIZE = 128
LOG2E = math.log2(math.e)

LANES = 128
VMEM_LIMIT = 56 * 1024 * 1024

C_CB, C_CC, C_CH, C_Q, C_KV, C_NG, C_MG, C_END = 0, 512, 1024, 1536, 2560, 3328, 3456, 5504

TQ = 256
KT = 256
HALF_KEYS = 8192
FAR_TILES = 4


def _dot(a, b):
    return jnp.dot(a, b, preferred_element_type=F32)


def _dot_nt(a, b):
    return lax.dot_general(a, b, (((1,), (1,)), ((), ())), preferred_element_type=F32)


def _cparams(sem):
    return pltpu.CompilerParams(dimension_semantics=sem, vmem_limit_bytes=VMEM_LIMIT)


def _layer_norm(x, g, b):
    mu = jnp.mean(x, axis=-1, keepdims=True)
    xc = x - mu
    var = jnp.mean(xc * xc, axis=-1, keepdims=True)
    return xc * lax.rsqrt(var + LN_EPS) * g + b


def _bucket_table():
    d = np.arange(0, MAX_DIST + 1)
    exact = N_BUCKETS // 2
    log_ratio = (np.log(np.maximum(d, 1).astype(np.float32) / np.float32(exact))
                 / np.float32(math.log(MAX_DIST / exact))).astype(np.float32)
    large = np.minimum(exact + (log_ratio * np.float32(N_BUCKETS - exact)).astype(np.int32), N_BUCKETS - 1)
    return np.where(d < exact, d, large).astype(np.int32)


def _bias_from_dist(bias_p, dist, causal):
    dc = np.clip(dist, 0, MAX_DIST)
    val = jnp.moveaxis(bias_p[dc], -1, 0)
    if causal:
        val = jnp.where(jnp.asarray(dist < 0), NEG, val)
    return val


def _toeplitz(bias_p, off, causal):
    n = TQ
    w = _bias_from_dist(bias_p, n - 1 - np.arange(2 * n - 1) + off, causal)
    w = jnp.pad(w, ((0, 0), (0, 1)))
    rows = jnp.tile(w, (1, n))[:, :n * (2 * n - 1)].reshape(w.shape[0], n, 2 * n - 1)
    return rows[:, :, n - 1:]


def _proj_kernel(*refs, tm, sample):
    if sample:
        (x_ref, w_ref, cw_ref, p0_ref, p1_ref, za_ref, q_ref, kcm_ref, vcm_ref, ksl_ref, vsl_ref, kwn_ref,
         vwn_ref, gates_ref, ga_ref, gb_ref, u_ref) = refs
    else:
        (x_ref, w_ref, cw_ref, za_ref, q_ref, kcm_ref, vcm_ref, ksl_ref, vsl_ref, kwn_ref, vwn_ref,
         kslb_ref, vslb_ref, kwnb_ref, vwnb_ref, gates_ref, ga_ref, gb_ref, conv_ref, ubuf) = refs
    xb = x_ref[...].astype(BF16)

    def mm(c0, c1):
        return _dot(xb, w_ref[:, c0:c1])

    u = mm(C_CB, C_CC) * mm(C_CH, C_Q)
    cc = mm(C_CC, C_CH)
    cw = cw_ref[...]
    if sample:
        u1 = p1_ref[...]
        u2 = p0_ref[...]
        u_ref[...] = u
    else:
        @pl.when(pl.program_id(1) == 0)
        def _():
            ubuf[0:8, :] = jnp.zeros((8, D_CONV), F32)
        ubuf[8:tm + 8, :] = u
        u1 = ubuf[7:tm + 7, :]
        u2 = ubuf[6:tm + 6, :]
        conv_ref[...] = ubuf[tm + 6:tm + 8, :]
        ubuf[0:8, :] = ubuf[tm:tm + 8, :]
    za_ref[...] = (cc * (cw[0:1] * u2 + cw[1:2] * u1 + cw[2:3] * u)).astype(BF16)
    q_ref[...] = mm(C_Q, C_KV).astype(BF16)
    kv = mm(C_KV, C_NG)
    for i, r in enumerate((kcm_ref, vcm_ref, ksl_ref, vsl_ref, kwn_ref, vwn_ref)):
        r[...] = kv[:, i * KVD:(i + 1) * KVD]
    if not sample:
        for i, r in enumerate((kslb_ref, vslb_ref, kwnb_ref, vwnb_ref)):
            r[...] = kv[:, (i + 2) * KVD:(i + 3) * KVD].astype(BF16)
    gates_ref[...] = jax.nn.sigmoid(mm(C_NG, C_MG))
    mg = jax.nn.sigmoid(mm(C_MG, C_END))
    ga_ref[...] = mg[:, :D_MODEL].astype(BF16)
    gb_ref[...] = mg[:, D_MODEL:].astype(BF16)


def _proj_call(x, w_perm, conv_w, prefix, tm):
    B, T, _ = x.shape
    sample = prefix is not None
    nt = T // tm
    row = lambda w: pl.BlockSpec((None, tm, w), lambda b, s: (b, s, 0))
    full = lambda a: pl.BlockSpec(a.shape, lambda b, s: (0,) * a.ndim)
    sds = lambda w, dt: jax.ShapeDtypeStruct((B, T, w), dt)
    in_specs = [row(D_MODEL), full(w_perm), full(conv_w)]
    args = [x, w_perm, conv_w]
    out_shape = [sds(D_CONV, BF16), sds(QD, BF16)] + [sds(KVD, F32)] * 6
    out_specs = [row(D_CONV), row(QD)] + [row(KVD)] * 6
    scratch = []
    if sample:
        in_specs += [row(D_CONV), row(D_CONV)]
        args += list(prefix)
    else:
        out_shape += [sds(KVD, BF16)] * 4
        out_specs += [row(KVD)] * 4
    out_shape += [sds(LANES, F32), sds(D_MODEL, BF16), sds(D_MODEL, BF16)]
    out_specs += [row(LANES), row(D_MODEL), row(D_MODEL)]
    if sample:
        out_shape.append(sds(D_CONV, F32))
        out_specs.append(row(D_CONV))
    else:
        out_shape.append(jax.ShapeDtypeStruct((B, CONV_W - 1, D_CONV), F32))
        out_specs.append(pl.BlockSpec((None, CONV_W - 1, D_CONV), lambda b, s: (b, 0, 0)))
        scratch.append(pltpu.VMEM((tm + 8, D_CONV), F32))
    return pl.pallas_call(
        functools.partial(_proj_kernel, tm=tm, sample=sample), name="proj",
        grid=(B, nt), in_specs=in_specs, out_specs=out_specs, out_shape=out_shape,
        scratch_shapes=scratch, compiler_params=_cparams(("arbitrary", "arbitrary")),
    )(*args)


CMP_ROWS = 256 * L_CMP


def _compress_rows(x_ref, pos_ref, w1_ref, w2_ref, tmp_ref):
    acc = None
    for lp in range(L_CMP // 2):
        xa = x_ref[pl.ds(2 * lp, 256, stride=L_CMP), :] + pos_ref[2 * lp]
        xb = x_ref[pl.ds(2 * lp + 1, 256, stride=L_CMP), :] + pos_ref[2 * lp + 1]
        t = _dot(jnp.concatenate([xa, xb], axis=1).astype(BF16), w1_ref[lp])
        acc = t if acc is None else acc + t
    hid = jax.nn.gelu(acc)
    tmp_ref[...] = _dot(hid.astype(BF16), w2_ref[...])
    return tmp_ref[pl.ds(0, 128, stride=2), :], tmp_ref[pl.ds(1, 128, stride=2), :]


def _compress_kernel(x_ref, pos_ref, w1_ref, w2_ref, o_ref, tmp_ref):
    ev, od = _compress_rows(x_ref, pos_ref, w1_ref, w2_ref, tmp_ref)
    o_ref[0] = ev.astype(BF16)
    o_ref[1] = od.astype(BF16)


def _compress_prompt(rows, pos2, w1p, w2p):
    B, S, _ = rows.shape
    nchunk = S // CMP_ROWS
    full = lambda a: pl.BlockSpec(a.shape, lambda b, i: (0,) * a.ndim)
    return pl.pallas_call(
        _compress_kernel, name="compress", grid=(B, nchunk),
        in_specs=[pl.BlockSpec((None, CMP_ROWS, KVD), lambda b, i: (b, i, 0)), full(pos2), full(w1p), full(w2p)],
        out_specs=pl.BlockSpec((None, 2, 128, KVD), lambda b, i: (b, 0, i, 0)),
        out_shape=jax.ShapeDtypeStruct((B, 2, nchunk * 128, KVD), BF16),
        scratch_shapes=[pltpu.VMEM((256, KVD), F32)],
        compiler_params=_cparams(("arbitrary", "arbitrary")),
    )(rows, pos2, w1p, w2p)


def _compress_pages_kernel(pt_ref, pool_ref, pos_ref, w1_ref, w2_ref, o_ref, buf, nat, tmp_ref, sem, *, n_pages):
    b = pl.program_id(0)
    nb = pl.num_programs(0)

    def copy(bb, p, slot):
        page = pt_ref[bb * n_pages + p]
        return pltpu.make_async_copy(pool_ref.at[page], buf.at[slot, pl.ds(PAGE_SIZE * p, PAGE_SIZE), :],
                                     sem.at[slot])

    def start_all(bb, slot):
        for p in range(n_pages):
            copy(bb, p, slot).start()

    slot = lax.rem(b, 2)

    @pl.when(b == 0)
    def _():
        start_all(b, 0)

    for p in range(n_pages):
        copy(b, p, slot).wait()

    @pl.when(b + 1 < nb)
    def _():
        start_all(b + 1, 1 - slot)

    for p in range(n_pages):
        rows = pl.ds(PAGE_SIZE * p, PAGE_SIZE)
        nat[rows, :] = buf[slot, rows, :].T
    ev, od = _compress_rows(nat, pos_ref, w1_ref, w2_ref, tmp_ref)
    o_ref[0] = ev.astype(BF16)
    o_ref[1] = od.astype(BF16)


def _compress_pages(page_table, pool_t, pos2, w1p, w2p):
    DB, n_pages = page_table.shape
    assert n_pages * PAGE_SIZE == CMP_ROWS and KVD == PAGE_SIZE
    full = lambda a: pl.BlockSpec(a.shape, lambda b, pt: (0,) * a.ndim)
    return pl.pallas_call(
        functools.partial(_compress_pages_kernel, n_pages=n_pages), name="compress_pages",
        grid_spec=pltpu.PrefetchScalarGridSpec(
            num_scalar_prefetch=1, grid=(DB,),
            in_specs=[pl.BlockSpec(memory_space=pl.ANY), full(pos2), full(w1p), full(w2p)],
            out_specs=pl.BlockSpec((None, 2, 128, KVD), lambda b, pt: (b, 0, 0, 0)),
            scratch_shapes=[pltpu.VMEM((2, CMP_ROWS, KVD), F32), pltpu.VMEM((CMP_ROWS, KVD), F32),
                            pltpu.VMEM((256, KVD), F32), pltpu.SemaphoreType.DMA((2,))]),
        out_shape=jax.ShapeDtypeStruct((DB, 2, 128, KVD), BF16),
        compiler_params=_cparams(("arbitrary",)),
    )(page_table.reshape(-1), pool_t, pos2, w1p, w2p)


def _topk_extract(v, k, want_idx, axis=1):
    pos = lax.broadcasted_iota(jnp.int32, v.shape, axis).astype(F32)
    idx = []
    for _ in range(k):
        mx = jnp.max(v, axis=axis, keepdims=True)
        first = jnp.min(jnp.where(v == mx, pos, 1e9), axis=axis, keepdims=True)
        v = jnp.where(pos == first, -jnp.inf, v)
        if want_idx:
            idx.append(first)
    return v, idx


def _cmp_kernel(q_ref, kc_ref, vc_ref, gates_ref, nb_ref, oc_ref, m_ref, *, nc):
    t0 = pl.program_id(1) * TQ
    half = nc // 2
    lane = lax.broadcasted_iota(jnp.int32, (nc, LANES), 1)
    crow = lax.broadcasted_iota(jnp.int32, (nc, LANES), 0)
    cnat = jnp.where(crow < half, 2 * crow, 2 * (crow - half) + 1)
    cbase = t0 // L_CMP - 8
    e = jnp.where(lane < 16, lane, lane - 16)
    ext = jnp.where((lane < 32) & (cnat == cbase + e), 1.0, 0.0)
    ext = jnp.where((lane == 32) & (cnat >= cbase + 16), NEG, ext).astype(BF16)
    lo = lane < HEAD_DIM
    kc = kc_ref[...]
    vc = vc_ref[...]
    zero = jnp.zeros_like(kc)
    rhs = [jnp.concatenate([jnp.where(lo, kc, zero), ext], axis=1),
           jnp.concatenate([jnp.where(lo, zero, kc), ext], axis=1)]
    vg = [jnp.where(lo, vc, zero), jnp.where(lo, zero, vc)]
    gates = gates_ref[...]
    imp = [jnp.zeros((TQ, nc), F32), jnp.zeros((TQ, nc), F32)]
    def scores(hh):
        qs = q_ref[:, (hh // 2) * LANES:(hh // 2 + 1) * LANES]
        return _dot_nt(jnp.concatenate([qs, nb_ref[hh]], axis=1), rhs[hh % 2])

    s_next = scores(0)
    osl = None
    for hh in range(N_HEADS):
        g = hh % 2
        s = s_next
        if hh + 1 < N_HEADS:
            s_next = scores(hh + 1)
        m = jnp.max(s, axis=-1, keepdims=True)
        ex = jnp.exp2(s - m)
        l = jnp.sum(ex, axis=-1, keepdims=True)
        p = ex * jnp.where(m > 0.1 * NEG, 1.0 / l, 0.0)
        imp[g] = imp[g] + p
        o = gates[:, hh:hh + 1] * _dot(p.astype(BF16), vg[g])
        if g == 0:
            osl = o
        else:
            oc_ref[:, (hh // 2) * LANES:(hh // 2 + 1) * LANES] = (osl + o).astype(BF16)
    blk = lax.broadcasted_iota(jnp.int32, (half, TQ), 0)
    tcol = t0 + lax.broadcasted_iota(jnp.int32, (half, TQ), 1)
    forced = (blk == tcol // L_SEL) | (blk == 0)
    started = blk * L_SEL <= tcol
    for g in range(N_KV):
        v = (imp[g][:, :half] + imp[g][:, half:]).T
        v = jnp.where(forced, FORCED_SCORE, jnp.where(started, v, -1.0))
        v, _ = _topk_extract(v, N_SEL, False, axis=0)
        m_ref[:, g * half:(g + 1) * half] = jnp.where(v == -jnp.inf, 0.0, NEG).T.astype(BF16)


def _cmp_call(q, kc, vc, gates, nbt):
    B, S, _ = q.shape
    nc = kc.shape[1]
    row = lambda w: pl.BlockSpec((None, TQ, w), lambda b, t: (b, t, 0))
    perb = lambda a: pl.BlockSpec((None,) + a.shape[1:], lambda b, t: (b,) + (0,) * (a.ndim - 1))
    full = lambda a: pl.BlockSpec(a.shape, lambda b, t: (0,) * a.ndim)
    return pl.pallas_call(
        functools.partial(_cmp_kernel, nc=nc), name="cmp", grid=(B, S // TQ),
        in_specs=[row(QD), perb(kc), perb(vc), row(LANES), full(nbt)],
        out_specs=[row(QD), row(nc)],
        out_shape=[jax.ShapeDtypeStruct((B, S, QD), BF16), jax.ShapeDtypeStruct((B, S, nc), BF16)],
        compiler_params=_cparams(("arbitrary", "arbitrary")),
    )(q, kc, vc, gates, nbt)


def _sel_kernel(q_ref, m_ref, gates_ref, k_ref, v_ref, oh_ref, bt_ref, os_ref, lhs_s, acc_s, m_s, *, nh):
    qt = pl.program_id(1)
    tph = HALF_KEYS // KT
    for j in range(GROUP):
        qs = q_ref[:, j * LANES:(j + 1) * LANES]
        for h in range(nh):
            for g in range(N_KV):
                c0 = (g * nh + h) * LANES
                lhs_s[(h * 2 + g) * GROUP + j] = jnp.concatenate([qs, m_ref[:, c0:c0 + LANES]], axis=1)
    m_s[...] = jnp.full(m_s.shape, NEG, F32)
    acc_s[...] = jnp.zeros(acc_s.shape, F32)
    lo_q = lax.broadcasted_iota(jnp.int32, (TQ, LANES), 1) < HEAD_DIM

    def step(kt, nk, kind):
        rows = nk * KT
        k_t = k_ref[pl.ds(pl.multiple_of(kt * KT, KT), rows), :]
        v_t = v_ref[pl.ds(pl.multiple_of(kt * KT, KT), rows), :]
        h = kt // tph
        oh = oh_ref[pl.ds(pl.multiple_of((kt - h * tph) * KT, KT), rows), :]
        lane = lax.broadcasted_iota(jnp.int32, (rows, LANES), 1)
        lo_k = lane < HEAD_DIM
        zero = jnp.zeros_like(k_t)
        rhs = [jnp.concatenate([jnp.where(lo_k, k_t, zero), oh], axis=1),
               jnp.concatenate([jnp.where(lo_k, zero, k_t), oh], axis=1)]
        vg = [jnp.where(lo_k, v_t, jnp.where(lane == HEAD_DIM, 1.0, 0.0).astype(BF16)),
              jnp.where(lo_k, jnp.where(lane == 0, 1.0, 0.0).astype(BF16), v_t)]
        def scores(hh):
            g = hh % 2
            return _dot_nt(lhs_s[(h * 2 + g) * GROUP + hh // 2], rhs[g])

        s_next = scores(0)
        for hh in range(N_HEADS):
            s = s_next
            if hh + 1 < N_HEADS:
                s_next = scores(hh + 1)
            if kind:
                s = s + bt_ref[kind - 1, hh].astype(F32)
            mp = m_s[hh]
            mc = jnp.broadcast_to(jnp.max(s, axis=-1, keepdims=True), (TQ, LANES))
            mn = jnp.maximum(mp, mc)
            m_s[hh] = mn
            p = jnp.exp2(s - jnp.concatenate([mn] * (rows // LANES), axis=1))
            acc_s[hh] = acc_s[hh] * jnp.exp2(mp - mn) + _dot(p.astype(BF16), vg[hh % 2])

    step(qt, 1, 2)

    @pl.when(qt >= 1)
    def _():
        step(qt - 1, 1, 1)

    nfar = jnp.maximum(qt - 1, 0)
    nbig = nfar // FAR_TILES

    def big(i, c):
        step(i * FAR_TILES, FAR_TILES, False)
        return c

    def small(kt, c):
        step(kt, 1, False)
        return c

    lax.fori_loop(0, nbig, big, 0)
    lax.fori_loop(nbig * FAR_TILES, nfar, small, 0)

    gates = gates_ref[...]
    for j in range(GROUP):
        a0 = acc_s[2 * j]
        a1 = acc_s[2 * j + 1]
        c = N_HEADS + 2 * j
        sc0 = gates[:, c:c + 1] / a0[:, HEAD_DIM:HEAD_DIM + 1]
        sc1 = gates[:, c + 1:c + 2] / a1[:, 0:1]
        os_ref[:, j * LANES:(j + 1) * LANES] = jnp.where(lo_q, a0 * sc0, a1 * sc1).astype(BF16)


def _sel_call(q, msk, gates, kb, vb, onehot, btab):
    B, S, _ = q.shape
    nh = msk.shape[2] // (2 * LANES)
    row = lambda w: pl.BlockSpec((None, TQ, w), lambda b, t: (b, t, 0))
    perb = lambda a: pl.BlockSpec((None,) + a.shape[1:], lambda b, t: (b,) + (0,) * (a.ndim - 1))
    full = lambda a: pl.BlockSpec(a.shape, lambda b, t: (0,) * a.ndim)
    return pl.pallas_call(
        functools.partial(_sel_kernel, nh=nh), name="sel", grid=(B, S // TQ),
        in_specs=[row(QD), row(msk.shape[2]), row(LANES), perb(kb), perb(vb), full(onehot), full(btab)],
        out_specs=row(QD),
        out_shape=jax.ShapeDtypeStruct((B, S, QD), BF16),
        scratch_shapes=[pltpu.VMEM((nh * 2 * GROUP, TQ, 2 * LANES), BF16), pltpu.VMEM((N_HEADS, TQ, LANES), F32),
                        pltpu.VMEM((N_HEADS, TQ, LANES), F32)],
        compiler_params=_cparams(("arbitrary", "arbitrary")),
    )(q, msk, gates, kb, vb, onehot, btab)


def _win_kernel(q_ref, gates_ref, k0_ref, k1_ref, k2_ref, v0_ref, v1_ref, v2_ref, bt_ref, ow_ref):
    qt = pl.program_id(1)
    lo_k = lax.broadcasted_iota(jnp.int32, (3 * KT, LANES), 1) < HEAD_DIM
    kcat = jnp.concatenate([k0_ref[...], k1_ref[...], k2_ref[...]], axis=0)
    vcat = jnp.concatenate([v0_ref[...], v1_ref[...], v2_ref[...]], axis=0)
    lane_k = lax.broadcasted_iota(jnp.int32, (3 * KT, LANES), 1)
    zero = jnp.zeros_like(kcat)
    kg = [jnp.where(lo_k, kcat, zero), jnp.where(lo_k, zero, kcat)]
    vg = [jnp.where(lo_k, vcat, jnp.where(lane_k == HEAD_DIM, 1.0, 0.0).astype(BF16)),
          jnp.where(lo_k, jnp.where(lane_k == 0, 1.0, 0.0).astype(BF16), vcat)]
    lo_q = lax.broadcasted_iota(jnp.int32, (TQ, LANES), 1) < HEAD_DIM
    ii = lax.broadcasted_iota(jnp.int32, (TQ, KT), 0)
    jj = lax.broadcasted_iota(jnp.int32, (TQ, KT), 1)
    pen0 = jnp.where((jj >= ii) & (qt >= 2), 0.0, NEG)
    pen1 = jnp.where(qt >= 1, 0.0, NEG)
    gates = gates_ref[...]
    def scores(hh):
        return _dot_nt(q_ref[:, (hh // 2) * LANES:(hh // 2 + 1) * LANES], kg[hh % 2])

    s_next = scores(0)
    pv = [None, None]
    for hh in range(N_HEADS):
        j, g = hh // 2, hh % 2
        s = s_next
        if hh + 1 < N_HEADS:
            s_next = scores(hh + 1)
        s = s + jnp.concatenate([pen0, bt_ref[0, hh].astype(F32) + pen1, bt_ref[1, hh].astype(F32)], axis=1)
        ex = jnp.exp2(s - jnp.max(s, axis=-1, keepdims=True))
        pv[g] = _dot(ex.astype(BF16), vg[g])
        if g == 1:
            c = 2 * N_HEADS + 2 * j
            sc0 = gates[:, c:c + 1] / pv[0][:, HEAD_DIM:HEAD_DIM + 1]
            sc1 = gates[:, c + 1:c + 2] / pv[1][:, 0:1]
            ow_ref[:, j * LANES:(j + 1) * LANES] = jnp.where(lo_q, pv[0] * sc0, pv[1] * sc1).astype(BF16)


def _win_call(q, gates, kb, vb, btab):
    B, S, _ = q.shape
    row = lambda w: pl.BlockSpec((None, TQ, w), lambda b, t: (b, t, 0))
    back = lambda d: pl.BlockSpec((None, KT, KVD), lambda b, t: (b, jnp.maximum(t - d, 0), 0))
    full = lambda a: pl.BlockSpec(a.shape, lambda b, t: (0,) * a.ndim)
    return pl.pallas_call(
        _win_kernel, name="win", grid=(B, S // TQ),
        in_specs=[row(QD), row(LANES), back(2), back(1), back(0), back(2), back(1), back(0), full(btab)],
        out_specs=row(QD),
        out_shape=jax.ShapeDtypeStruct((B, S, QD), BF16),
        compiler_params=_cparams(("arbitrary", "arbitrary")),
    )(q, gates, kb, kb, kb, vb, vb, vb, btab)


def _merge_kernel(za_ref, oc_ref, os_ref, ow_ref, ga_ref, gb_ref, x_ref, wuc_ref, wua_ref, wo_ref, g_ref, b_ref,
                  h_ref):
    a = _dot(za_ref[...], wuc_ref[...])
    ob = oc_ref[...].astype(F32) + os_ref[...].astype(F32) + ow_ref[...].astype(F32)
    bm = _dot(ob.astype(BF16), wua_ref[...])
    pre = ga_ref[...].astype(F32) * a + gb_ref[...].astype(F32) * bm
    mix = _dot(pre.astype(BF16), wo_ref[...])
    h_ref[...] = _layer_norm(ALPHA * x_ref[...] + mix, g_ref[...], b_ref[...])


def _merge_call(za, oc, os_, ow, ga, gb, x, wuc, wua, wo, g, b, tm):
    R = x.shape[0]
    row = lambda w: pl.BlockSpec((tm, w), lambda i: (i, 0))
    full = lambda a: pl.BlockSpec(a.shape, lambda i: (0,) * a.ndim)
    return pl.pallas_call(
        _merge_kernel, name="merge", grid=(R // tm,),
        in_specs=[row(D_CONV), row(QD), row(QD), row(QD), row(D_MODEL), row(D_MODEL), row(D_MODEL),
                  full(wuc), full(wua), full(wo), full(g), full(b)],
        out_specs=row(D_MODEL), out_shape=jax.ShapeDtypeStruct((R, D_MODEL), F32),
        compiler_params=_cparams(("arbitrary",)),
    )(za, oc, os_, ow, ga, gb, x, wuc, wua, wo, g, b)


def _ffn_kernel(h_ref, wg_ref, wu_ref, wd_ref, g_ref, b_ref, y_ref):
    h = h_ref[...]
    hb = h.astype(BF16)
    act = jax.nn.silu(_dot(hb, wg_ref[...])) * _dot(hb, wu_ref[...])
    f = _dot(act.astype(BF16), wd_ref[...])
    y_ref[...] = _layer_norm(ALPHA * h + f, g_ref[...], b_ref[...])


def _ffn_call(h, wg, wu, wd, g, b, tm):
    R = h.shape[0]
    row = lambda w: pl.BlockSpec((tm, w), lambda i: (i, 0))
    full = lambda a: pl.BlockSpec(a.shape, lambda i: (0,) * a.ndim)
    return pl.pallas_call(
        _ffn_kernel, name="ffn", grid=(R // tm,),
        in_specs=[row(D_MODEL), full(wg), full(wu), full(wd), full(g), full(b)],
        out_specs=row(D_MODEL), out_shape=jax.ShapeDtypeStruct((R, D_MODEL), F32),
        compiler_params=_cparams(("arbitrary",)),
    )(h, wg, wu, wd, g, b)


def _scmp_kernel(qh_ref, kc_ref, vc_ref, gate_ref, bc_ref, oc_ref, imp_ref, *, nc):
    half = nc // 2
    s = _dot_nt(qh_ref[...], kc_ref[...]) + bc_ref[...]
    m = jnp.max(s, axis=-1, keepdims=True)
    ex = jnp.exp2(s - m)
    p = ex * (1.0 / jnp.sum(ex, axis=-1, keepdims=True))
    o = _dot(p.astype(BF16), vc_ref[...]) * gate_ref[...]
    oc_ref[...] = o[0:GROUP] + o[GROUP:2 * GROUP]
    for g in range(N_KV):
        pg = jnp.sum(p[g * GROUP:(g + 1) * GROUP], axis=0, keepdims=True)
        imp_ref[g:g + 1, :] = pg[:, :half] + pg[:, half:]


def _scmp_call(qh, kc, vc, gate_c, bc):
    DB, _, nc, _ = (kc.shape[0], None, kc.shape[1], None)
    perb = lambda a: pl.BlockSpec((None,) + a.shape[1:], lambda b: (b,) + (0,) * (a.ndim - 1))
    full = lambda a: pl.BlockSpec(a.shape, lambda b: (0,) * a.ndim)
    return pl.pallas_call(
        functools.partial(_scmp_kernel, nc=nc), name="sample_cmp", grid=(DB,),
        in_specs=[perb(qh), perb(kc), perb(vc), perb(gate_c), full(bc)],
        out_specs=[pl.BlockSpec((None, GROUP, LANES), lambda b: (b, 0, 0)),
                   pl.BlockSpec((None, N_KV, nc // 2), lambda b: (b, 0, 0))],
        out_shape=[jax.ShapeDtypeStruct((DB, GROUP, LANES), F32), jax.ShapeDtypeStruct((DB, N_KV, nc // 2), F32)],
        compiler_params=_cparams(("arbitrary",)),
    )(qh, kc, vc, gate_c, bc)


def _stopk_kernel(imp_ref, idx_ref, *, k):
    v = imp_ref[...]
    lane = lax.broadcasted_iota(jnp.int32, v.shape, 1)
    v = jnp.where(lane == 0, FORCED_SCORE, v)
    _, idx = _topk_extract(v, k, True)
    out = jnp.zeros(v.shape, F32)
    for i, col in enumerate(idx):
        out = jnp.where(lane == i, col, out)
    idx_ref[...] = out.astype(jnp.int32)


def _stopk_call(imp, k):
    return pl.pallas_call(
        functools.partial(_stopk_kernel, k=k), name="sample_topk",
        out_shape=jax.ShapeDtypeStruct(imp.shape, jnp.int32),
        compiler_params=pltpu.CompilerParams(vmem_limit_bytes=VMEM_LIMIT),
    )(imp)


def _ssel_kernel(idx_ref, pt_ref, q_ref, knew_ref, vnew_ref, kwnew_ref, vwnew_ref, kwcol_ref, vwcol_ref,
                 gate_s_ref, gate_w_ref, kwin_ref, vwin_ref, tn_ref, tw_ref, b0_ref, kpool_ref, vpool_ref,
                 os_ref, ow_ref, kwo_ref, vwo_ref, kbuf, vbuf, sem, *, n_pages, nsel, n_last):
    b = pl.program_id(0)

    def copies(g, s):
        n = idx_ref[(b * N_KV + g) * nsel + s]
        page = pt_ref[b * n_pages + n // 2]
        dst = pl.ds(s * PAGE_SIZE, PAGE_SIZE)
        return (pltpu.make_async_copy(kpool_ref.at[page, g], kbuf.at[g, :, dst], sem.at[0]),
                pltpu.make_async_copy(vpool_ref.at[page, g], vbuf.at[g, :, dst], sem.at[1]))

    for g in range(N_KV):
        for s in range(nsel):
            ck, cv = copies(g, s)
            ck.start()
            cv.start()

    def attend(qg, s, k_new, v_new, vt, bias_new, gate):
        s_new = jnp.sum(qg.astype(F32) * k_new.astype(BF16).astype(F32), axis=-1, keepdims=True) + bias_new
        m = jnp.maximum(jnp.max(s, axis=-1, keepdims=True), s_new)
        ex = jnp.exp2(s - m)
        ex_new = jnp.exp2(s_new - m)
        l = jnp.sum(ex, axis=-1, keepdims=True) + ex_new
        o = _dot_nt(ex.astype(BF16), vt.astype(BF16)) + ex_new.astype(BF16).astype(F32) * v_new.astype(BF16).astype(F32)
        return o * (gate / l)

    wb = kwin_ref.shape[-1]
    lane_w = lax.broadcasted_iota(jnp.int32, (HEAD_DIM, wb), 1)
    for g in range(N_KV):
        qg = q_ref[g]
        kw = kwin_ref[g]
        vw = vwin_ref[g]
        s = _dot(qg, kw.astype(BF16)) + tw_ref[g]
        ow_ref[g] = attend(qg, s, kwnew_ref[g], vwnew_ref[g], vw, b0_ref[g], gate_w_ref[g])
        kwo_ref[g] = jnp.where(lane_w == wb - 1, kwcol_ref[g], pltpu.roll(kw, wb - 1, 1))
        vwo_ref[g] = jnp.where(lane_w == wb - 1, vwcol_ref[g], pltpu.roll(vw, wb - 1, 1))

    for g in range(N_KV):
        for s in range(nsel):
            ck, cv = copies(g, s)
            ck.wait()
            cv.wait()

    upper = lax.broadcasted_iota(jnp.int32, (GROUP, PAGE_SIZE), 1) >= L_SEL
    keep_upper = jnp.where(upper, 0.0, NEG)
    keep_lower = jnp.where(upper, NEG, 0.0)
    for g in range(N_KV):
        qg = q_ref[g]
        pieces = []
        for s in range(nsel):
            n = idx_ref[(b * N_KV + g) * nsel + s]
            piece = jnp.where(n % 2 == 1, keep_upper, keep_lower)
            piece = piece + jnp.where(n == n_last, 1.0, 0.0) * tn_ref[0, g]
            piece = piece + jnp.where(n == n_last - 1, 1.0, 0.0) * tn_ref[1, g]
            pieces.append(piece)
        s = _dot(qg, kbuf[g].astype(BF16)) + jnp.concatenate(pieces, axis=1)
        os_ref[g] = attend(qg, s, knew_ref[g], vnew_ref[g], vbuf[g], b0_ref[g], gate_s_ref[g])


def _ssel_call(idx, page_table, qg, knew, vnew, kwnew, vwnew, kwcol, vwcol, gate_s, gate_w, kwin, vwin, tn, tw, b0,
               kpool, vpool):
    DB, n_pages = page_table.shape
    nsel = idx.shape[-1]
    wb = kwin.shape[-1]
    n_last = n_pages * (PAGE_SIZE // L_SEL) - 1
    perb = lambda a: pl.BlockSpec((None,) + a.shape[1:], lambda b, *_: (b,) + (0,) * (a.ndim - 1))
    full = lambda a: pl.BlockSpec(a.shape, lambda b, *_: (0,) * a.ndim)
    anyspec = pl.BlockSpec(memory_space=pl.ANY)
    head_o = pl.BlockSpec((None, N_KV, GROUP, HEAD_DIM), lambda b, *_: (b, 0, 0, 0))
    win_o = pl.BlockSpec((None, N_KV, HEAD_DIM, wb), lambda b, *_: (b, 0, 0, 0))
    head_s = jax.ShapeDtypeStruct((DB, N_KV, GROUP, HEAD_DIM), F32)
    win_s = jax.ShapeDtypeStruct((DB, N_KV, HEAD_DIM, wb), F32)
    return pl.pallas_call(
        functools.partial(_ssel_kernel, n_pages=n_pages, nsel=nsel, n_last=n_last), name="sample_sel_win",
        grid_spec=pltpu.PrefetchScalarGridSpec(
            num_scalar_prefetch=2, grid=(DB,),
            in_specs=[perb(qg), perb(knew), perb(vnew), perb(kwnew), perb(vwnew), perb(kwcol), perb(vwcol),
                      perb(gate_s), perb(gate_w), perb(kwin), perb(vwin), full(tn), full(tw), full(b0),
                      anyspec, anyspec],
            out_specs=[head_o, head_o, win_o, win_o],
            scratch_shapes=[pltpu.VMEM((N_KV, HEAD_DIM, nsel * PAGE_SIZE), F32),
                            pltpu.VMEM((N_KV, HEAD_DIM, nsel * PAGE_SIZE), F32),
                            pltpu.SemaphoreType.DMA((2,))]),
        out_shape=[head_s, head_s, win_s, win_s],
        compiler_params=_cparams(("arbitrary",)),
    )(idx.reshape(-1), page_table.reshape(-1), qg, knew, vnew, kwnew, vwnew, kwcol, vwcol, gate_s, gate_w,
      kwin, vwin, tn, tw, b0, kpool, vpool)


def _perm_w_in(w_in):
    o_q = 3 * D_CONV
    o_kv = o_q + QD
    o_ng = o_kv + 6 * KVD
    o_mg = o_ng + 3 * N_HEADS
    q = w_in[:, o_q:o_kv].reshape(D_MODEL, N_KV, GROUP, HEAD_DIM).transpose(0, 2, 1, 3).reshape(D_MODEL, QD)
    q = q * (HEAD_DIM ** -0.5 * LOG2E)
    ng = w_in[:, o_ng:o_mg].reshape(D_MODEL, N_KV, GROUP, 3).transpose(0, 3, 2, 1).reshape(D_MODEL, 3 * N_HEADS)
    ng = jnp.pad(ng, ((0, 0), (0, LANES - 3 * N_HEADS)))
    return jnp.concatenate([w_in[:, :o_q], q, w_in[:, o_kv:o_ng], ng, w_in[:, o_mg:]], axis=1).astype(BF16)


def _perm_heads(t):
    return t.reshape(t.shape[:-1] + (N_KV, GROUP)).swapaxes(-1, -2).reshape(t.shape)


def _cmp_weights(w1, w2, pos):
    eye = jnp.eye(N_KV, dtype=F32)
    w1p = jnp.einsum('ldh,ab->ladbh', w1.reshape(L_CMP, HEAD_DIM, CMP_HID), eye)
    w1p = w1p.reshape(L_CMP // 2, 2 * KVD, N_KV * CMP_HID).astype(BF16)
    w2p = jnp.einsum('hd,ab->ahbd', w2, eye).reshape(N_KV * CMP_HID, KVD).astype(BF16)
    pos2 = jnp.tile(pos[:, None, :], (1, N_KV, 1)).reshape(L_CMP, 1, KVD)
    return pos2, w1p, w2p


def kernel(x_prompt, x_sample, cache_k_cmp, cache_v_cmp, cache_k_slc, cache_v_slc, state_k_win, state_v_win,
           state_conv, page_table, w_in, conv_w, w_cmp_k1, w_cmp_k2, pos_cmp_k, w_cmp_v1, w_cmp_v2, pos_cmp_v,
           rel_bias, w_up_conv, w_up_attn, w_o, ln1_g, ln1_b, w_gate_ffn, w_up_ffn, w_down_ffn, ln2_g, ln2_b):
    B, S, _ = x_prompt.shape
    DB = x_sample.shape[0]
    n_pages = page_table.shape[1]
    P = n_pages * PAGE_SIZE
    n_pool = cache_k_cmp.shape[1]
    wb = state_k_win.shape[2]
    assert x_sample.shape[1] == 1 and w_in.shape[0] == 1
    assert S % HALF_KEYS == 0 and P == HALF_KEYS and wb == WINDOW and S >= WINDOW
    nc = S // L_CMP
    nh = S // HALF_KEYS

    w_perm = _perm_w_in(w_in[0])
    cw = conv_w[0]
    cmp_k = _cmp_weights(w_cmp_k1[0], w_cmp_k2[0], pos_cmp_k[0])
    cmp_v = _cmp_weights(w_cmp_v1[0], w_cmp_v2[0], pos_cmp_v[0])
    wuc = w_up_conv[0].astype(BF16)
    wua = w_up_attn[0].reshape(N_KV, GROUP, HEAD_DIM, D_MODEL).transpose(1, 0, 2, 3).reshape(QD, D_MODEL).astype(BF16)
    wo = w_o[0].astype(BF16)
    wg, wu, wd = w_gate_ffn[0].astype(BF16), w_up_ffn[0].astype(BF16), w_down_ffn[0].astype(BF16)
    g1, b1, g2, b2 = ln1_g, ln1_b, ln2_g, ln2_b

    bucket = _bucket_table()
    bias_p = (rel_bias[bucket] - rel_bias[N_BUCKETS - 1][None, :]) * LOG2E
    bias_pp = _perm_heads(bias_p)
    ii = np.arange(TQ)[:, None]
    dist_c = ii - L_CMP * np.arange(16)[None, :] + 8 * L_CMP - (L_CMP - 1)
    nb = _bias_from_dist(bias_pp, dist_c, True)
    nb_hi = nb.astype(BF16)
    nb_lo = jnp.where(jnp.asarray(dist_c < 0), 0.0, nb - nb_hi.astype(F32)).astype(BF16)
    nbt = jnp.concatenate([nb_hi, nb_lo, jnp.ones((N_HEADS, TQ, 1), BF16),
                           jnp.zeros((N_HEADS, TQ, LANES - 33), BF16)], axis=-1)
    btab = jnp.stack([_toeplitz(bias_pp, KT, False), _toeplitz(bias_pp, 0, True)]).astype(BF16)
    onehot = jnp.asarray((np.arange(HALF_KEYS)[:, None] // L_SEL == np.arange(LANES)[None, :]), BF16)

    (za, q, kcm, vcm, ksl, vsl, kwn, vwn, kslb, vslb, kwnb, vwnb, gates, ga, gb, conv_p) = _proj_call(
        x_prompt, w_perm, cw, None, 256)
    kc = _compress_prompt(kcm, *cmp_k).reshape(B, nc, KVD)
    vc = _compress_prompt(vcm, *cmp_v).reshape(B, nc, KVD)
    oc, msk = _cmp_call(q, kc, vc, gates, nbt)
    os_ = _sel_call(q, msk, gates, kslb, vslb, onehot, btab)
    ow = _win_call(q, gates, kwnb, vwnb, btab)
    f2 = lambda t: t.reshape(B * S, t.shape[-1])
    h = _merge_call(f2(za), f2(oc), f2(os_), f2(ow), f2(ga), f2(gb), f2(x_prompt), wuc, wua, wo, g1, b1, 256)
    y_p = _ffn_call(h, wg, wu, wd, g2, b2, 256).reshape(B, S, D_MODEL)
    kv5 = lambda t: t.reshape(1, B, S, N_KV, HEAD_DIM)
    keep = min(WINDOW, S)
    prompt_state = (kv5(kcm), kv5(vcm), kv5(ksl), kv5(vsl), kv5(kwn)[:, :, S - keep:], kv5(vwn)[:, :, S - keep:],
                    conv_p[None])

    xs = x_sample.reshape(1, DB, D_MODEL)
    p0 = state_conv[0, :, 0, :][None]
    p1 = state_conv[0, :, 1, :][None]
    (za_s, q_s, kcm_s, vcm_s, ksl_s, vsl_s, kwn_s, vwn_s, gates_s, ga_s, gb_s, u_s) = _proj_call(
        xs, w_perm, cw, (p0, p1), DB)
    pool_rows = lambda c: jnp.transpose(c[0], (0, 2, 3, 1)).reshape(n_pool, KVD, PAGE_SIZE)
    kc_s = _compress_pages(page_table, pool_rows(cache_k_cmp), *cmp_k).reshape(DB, P // L_CMP, KVD)
    vc_s = _compress_pages(page_table, pool_rows(cache_v_cmp), *cmp_v).reshape(DB, P // L_CMP, KVD)
    lane_lo = jnp.arange(LANES) < HEAD_DIM
    q3 = q_s.reshape(DB, GROUP, LANES)
    qh = jnp.concatenate([jnp.where(lane_lo, q3, 0), jnp.where(lane_lo, 0, q3)], axis=1)
    gsm = gates_s.reshape(DB, LANES)[:, :3 * N_HEADS].reshape(DB, 3, GROUP, N_KV)
    gsm = gsm.transpose(1, 0, 3, 2).reshape(3, DB, N_HEADS, 1)
    row_lo = (jnp.arange(N_HEADS) < GROUP)[:, None]
    lane_own = jnp.where(row_lo, lane_lo[None, :], ~lane_lo[None, :])
    gate3 = jnp.where(lane_own, gsm, 0.0)
    ncs = P // L_CMP
    cnat = np.concatenate([np.arange(0, ncs, 2), np.arange(1, ncs, 2)])
    bc = _bias_from_dist(bias_p, P - (L_CMP * cnat + L_CMP - 1), False)
    oc_s, imp_s = _scmp_call(qh, kc_s, vc_s, gate3[0], bc)
    nsel = N_SEL - 1
    idx = _stopk_call(imp_s.reshape(DB * N_KV, ncs // 2), nsel)[:, :nsel]
    hg = lambda t: t.reshape((N_KV, GROUP) + t.shape[1:])
    zpad = jnp.zeros((N_HEADS, L_SEL), F32)
    t_last = _bias_from_dist(bias_p, L_SEL - np.arange(L_SEL), False)
    t_prev = _bias_from_dist(bias_p, 2 * L_SEL - np.arange(L_SEL), False)
    tn = jnp.stack([hg(jnp.concatenate([zpad, t_last], 1)), hg(jnp.concatenate([t_prev, zpad], 1))])
    tw = hg(_bias_from_dist(bias_p, wb - np.arange(wb), False))
    b0 = hg(bias_p[0][:, None])
    qg = q_s.reshape(DB, GROUP, N_KV, HEAD_DIM).transpose(0, 2, 1, 3)
    gate_g = gsm.reshape(3, DB, N_KV, GROUP, 1)
    rowv = lambda t: t.reshape(DB, N_KV, 1, HEAD_DIM)
    colv = lambda t: t.reshape(DB, N_KV, HEAD_DIM, 1)
    tpose = lambda t: jnp.transpose(t[0], (0, 2, 3, 1))
    os_s, ow_s, kw_out, vw_out = _ssel_call(
        idx, page_table, qg, rowv(ksl_s), rowv(vsl_s), rowv(kwn_s), rowv(vwn_s), colv(kwn_s), colv(vwn_s),
        gate_g[1], gate_g[2], tpose(state_k_win), tpose(state_v_win), tn, tw, b0,
        tpose(cache_k_slc), tpose(cache_v_slc))
    fl = lambda t: t.reshape(DB, -1)
    slots = lambda t: t.transpose(0, 2, 1, 3).reshape(DB, QD)
    h_s = _merge_call(fl(za_s), fl(oc_s), slots(os_s), slots(ow_s), fl(ga_s), fl(gb_s), fl(xs), wuc, wua, wo,
                      g1, b1, DB)
    y_s = _ffn_call(h_s, wg, wu, wd, g2, b2, DB).reshape(DB, 1, D_MODEL)
    kv5s = lambda t: t.reshape(1, DB, 1, N_KV, HEAD_DIM)
    conv_s = jnp.stack([state_conv[0, :, 1, :], u_s.reshape(DB, D_CONV)], axis=1)[None]
    untp = lambda t: jnp.transpose(t, (0, 3, 1, 2))[None]
    sample_state = (kv5s(kcm_s), kv5s(vcm_s), kv5s(ksl_s), kv5s(vsl_s), untp(kw_out), untp(vw_out), conv_s)
    return (y_p, y_s) + prompt_state + sample_state
```

```python
import functools
import math

import numpy as np
import jax
import jax.numpy as jnp
from jax import lax
from jax.experimental import pallas as pl
from jax.experimental.pallas import tpu as pltpu

F32 = jnp.float32
BF16 = jnp.bfloat16

D_MODEL = 1024
D_CONV = 512
CONV_W = 3
N_HEADS = 16
HEAD_DIM = 64
N_KV = 2
GROUP = N_HEADS // N_KV
QD = N_HEADS * HEAD_DIM
KVD = N_KV * HEAD_DIM
L_CMP = 32
L_SEL = 64
N_SEL = 16
WINDOW = 512
CMP_HID = 256
FORCED_SCORE = 1e9
N_BUCKETS = 32
MAX_DIST = 128
D_FF = -(-8 * D_MODEL // (3 * 256)) * 256
DEPTH = 1
ALPHA = (2 * DEPTH) ** 0.25
LN_EPS = 1e-5
NEG = -1e30
PAGE_SIZE = 128
LOG2E = math.log2(math.e)

LANES = 128
VMEM_LIMIT = 56 * 1024 * 1024

C_CB, C_CC, C_CH, C_Q, C_KV, C_NG, C_MG, C_END = 0, 512, 1024, 1536, 2560, 3328, 3456, 5504

TQ = 256
KT = 256
HALF_KEYS = 8192
FAR_TILES = 4
SUB = 128


def _dot(a, b):
    return jnp.dot(a, b, preferred_element_type=F32)


def _dot_nt(a, b):
    return lax.dot_general(a, b, (((1,), (1,)), ((), ())), preferred_element_type=F32)


def _cparams(sem):
    return pltpu.CompilerParams(dimension_semantics=sem, vmem_limit_bytes=VMEM_LIMIT)


def _layer_norm(x, g, b):
    mu = jnp.mean(x, axis=-1, keepdims=True)
    xc = x - mu
    var = jnp.mean(xc * xc, axis=-1, keepdims=True)
    return xc * lax.rsqrt(var + LN_EPS) * g + b


def _bucket_table():
    d = np.arange(0, MAX_DIST + 1)
    exact = N_BUCKETS // 2
    log_ratio = (np.log(np.maximum(d, 1).astype(np.float32) / np.float32(exact))
                 / np.float32(math.log(MAX_DIST / exact))).astype(np.float32)
    large = np.minimum(exact + (log_ratio * np.float32(N_BUCKETS - exact)).astype(np.int32), N_BUCKETS - 1)
    return np.where(d < exact, d, large).astype(np.int32)


def _bias_from_dist(bias_p, dist, causal):
    dc = np.clip(dist, 0, MAX_DIST)
    val = jnp.moveaxis(bias_p[dc], -1, 0)
    if causal:
        val = jnp.where(jnp.asarray(dist < 0), NEG, val)
    return val


def _toeplitz(bias_p, off, causal, n=TQ):
    w = _bias_from_dist(bias_p, n - 1 - np.arange(2 * n - 1) + off, causal)
    w = jnp.pad(w, ((0, 0), (0, 1)))
    rows = jnp.tile(w, (1, n))[:, :n * (2 * n - 1)].reshape(w.shape[0], n, 2 * n - 1)
    return rows[:, :, n - 1:]


def _proj_kernel(*refs, tm, sample):
    if sample:
        (x_ref, w_ref, cw_ref, p0_ref, p1_ref, za_ref, q_ref, kcm_ref, vcm_ref, ksl_ref, vsl_ref, kwn_ref,
         vwn_ref, gates_ref, ga_ref, gb_ref, u_ref) = refs
    else:
        (x_ref, w_ref, cw_ref, za_ref, q_ref, kcm_ref, vcm_ref, ksl_ref, vsl_ref, kwn_ref, vwn_ref,
         kslb_ref, vslb_ref, kwnb_ref, vwnb_ref, gates_ref, ga_ref, gb_ref, conv_ref, ubuf) = refs
    xb = x_ref[...].astype(BF16)

    def mm(c0, c1):
        return _dot(xb, w_ref[:, c0:c1])

    u = mm(C_CB, C_CC) * mm(C_CH, C_Q)
    cc = mm(C_CC, C_CH)
    cw = cw_ref[...]
    if sample:
        u1 = p1_ref[...]
        u2 = p0_ref[...]
        u_ref[...] = u
    else:
        @pl.when(pl.program_id(1) == 0)
        def _():
            ubuf[0:8, :] = jnp.zeros((8, D_CONV), F32)
        ubuf[8:tm + 8, :] = u
        u1 = ubuf[7:tm + 7, :]
        u2 = ubuf[6:tm + 6, :]
        conv_ref[...] = ubuf[tm + 6:tm + 8, :]
        ubuf[0:8, :] = ubuf[tm:tm + 8, :]
    za_ref[...] = (cc * (cw[0:1] * u2 + cw[1:2] * u1 + cw[2:3] * u)).astype(BF16)
    q_ref[...] = mm(C_Q, C_KV).astype(BF16)
    kv = mm(C_KV, C_NG)
    for i, r in enumerate((kcm_ref, vcm_ref, ksl_ref, vsl_ref, kwn_ref, vwn_ref)):
        r[...] = kv[:, i * KVD:(i + 1) * KVD]
    if not sample:
        for i, r in enumerate((kslb_ref, vslb_ref, kwnb_ref, vwnb_ref)):
            r[...] = kv[:, (i + 2) * KVD:(i + 3) * KVD].astype(BF16)
    gates_ref[...] = jax.nn.sigmoid(mm(C_NG, C_MG))
    mg = jax.nn.sigmoid(mm(C_MG, C_END))
    ga_ref[...] = mg[:, :D_MODEL].astype(BF16)
    gb_ref[...] = mg[:, D_MODEL:].astype(BF16)


def _proj_call(x, w_perm, conv_w, prefix, tm):
    B, T, _ = x.shape
    sample = prefix is not None
    nt = T // tm
    row = lambda w: pl.BlockSpec((None, tm, w), lambda b, s: (b, s, 0))
    full = lambda a: pl.BlockSpec(a.shape, lambda b, s: (0,) * a.ndim)
    sds = lambda w, dt: jax.ShapeDtypeStruct((B, T, w), dt)
    in_specs = [row(D_MODEL), full(w_perm), full(conv_w)]
    args = [x, w_perm, conv_w]
    out_shape = [sds(D_CONV, BF16), sds(QD, BF16)] + [sds(KVD, F32)] * 6
    out_specs = [row(D_CONV), row(QD)] + [row(KVD)] * 6
    scratch = []
    if sample:
        in_specs += [row(D_CONV), row(D_CONV)]
        args += list(prefix)
    else:
        out_shape += [sds(KVD, BF16)] * 4
        out_specs += [row(KVD)] * 4
    out_shape += [sds(LANES, F32), sds(D_MODEL, BF16), sds(D_MODEL, BF16)]
    out_specs += [row(LANES), row(D_MODEL), row(D_MODEL)]
    if sample:
        out_shape.append(sds(D_CONV, F32))
        out_specs.append(row(D_CONV))
    else:
        out_shape.append(jax.ShapeDtypeStruct((B, CONV_W - 1, D_CONV), F32))
        out_specs.append(pl.BlockSpec((None, CONV_W - 1, D_CONV), lambda b, s: (b, 0, 0)))
        scratch.append(pltpu.VMEM((tm + 8, D_CONV), F32))
    return pl.pallas_call(
        functools.partial(_proj_kernel, tm=tm, sample=sample), name="proj",
        grid=(B, nt), in_specs=in_specs, out_specs=out_specs, out_shape=out_shape,
        scratch_shapes=scratch, compiler_params=_cparams(("arbitrary", "arbitrary")),
    )(*args)


CMP_ROWS = 256 * L_CMP


BLOCK_PITCH = 40


def _compress_rows(x_ref, pos_ref, w1_ref, w2_ref, tmp_ref, pitch):
    acc = None
    for lp in range(L_CMP // 2):
        xa = x_ref[pl.ds(2 * lp, 256, stride=pitch), :] + pos_ref[2 * lp]
        xb = x_ref[pl.ds(2 * lp + 1, 256, stride=pitch), :] + pos_ref[2 * lp + 1]
        t = _dot(jnp.concatenate([xa, xb], axis=1).astype(BF16), w1_ref[lp])
        acc = t if acc is None else acc + t
    hid = jax.nn.gelu(acc)
    tmp_ref[...] = _dot(hid.astype(BF16), w2_ref[...])
    return tmp_ref[pl.ds(0, 128, stride=2), :], tmp_ref[pl.ds(1, 128, stride=2), :]


def _compress_kernel(x_ref, pos_ref, w1_ref, w2_ref, o_ref, tmp_ref):
    ev, od = _compress_rows(x_ref, pos_ref, w1_ref, w2_ref, tmp_ref, L_CMP)
    o_ref[0] = ev.astype(BF16)
    o_ref[1] = od.astype(BF16)


def _compress_prompt(rows, pos2, w1p, w2p):
    B, S, _ = rows.shape
    nchunk = S // CMP_ROWS
    full = lambda a: pl.BlockSpec(a.shape, lambda b, i: (0,) * a.ndim)
    return pl.pallas_call(
        _compress_kernel, name="compress", grid=(B, nchunk),
        in_specs=[pl.BlockSpec((None, CMP_ROWS, KVD), lambda b, i: (b, i, 0)), full(pos2), full(w1p), full(w2p)],
        out_specs=pl.BlockSpec((None, 2, 128, KVD), lambda b, i: (b, 0, i, 0)),
        out_shape=jax.ShapeDtypeStruct((B, 2, nchunk * 128, KVD), BF16),
        scratch_shapes=[pltpu.VMEM((256, KVD), F32)],
        compiler_params=_cparams(("arbitrary", "arbitrary")),
    )(rows, pos2, w1p, w2p)


def _compress_pages_kernel(pt_ref, pool_ref, pos_ref, w1_ref, w2_ref, o_ref, buf, nat, tmp_ref, sem, *, n_pages):
    b = pl.program_id(0)
    nb = pl.num_programs(0)

    def copy(bb, p, slot):
        page = pt_ref[bb * n_pages + p]
        return pltpu.make_async_copy(pool_ref.at[page], buf.at[slot, pl.ds(PAGE_SIZE * p, PAGE_SIZE), :],
                                     sem.at[slot])

    def start_all(bb, slot):
        for p in range(n_pages):
            copy(bb, p, slot).start()

    slot = lax.rem(b, 2)

    @pl.when(b == 0)
    def _():
        start_all(b, 0)

    for p in range(n_pages):
        copy(b, p, slot).wait()

    @pl.when(b + 1 < nb)
    def _():
        start_all(b + 1, 1 - slot)

    per_page = PAGE_SIZE // L_CMP
    for p in range(n_pages):
        t = buf[slot, pl.ds(PAGE_SIZE * p, PAGE_SIZE), :].T
        for n in range(per_page):
            nat[pl.ds((per_page * p + n) * BLOCK_PITCH, L_CMP), :] = t[n * L_CMP:(n + 1) * L_CMP, :]
    ev, od = _compress_rows(nat, pos_ref, w1_ref, w2_ref, tmp_ref, BLOCK_PITCH)
    o_ref[0] = ev.astype(BF16)
    o_ref[1] = od.astype(BF16)


def _compress_pages(page_table, pool_t, pos2, w1p, w2p):
    DB, n_pages = page_table.shape
    assert n_pages * PAGE_SIZE == CMP_ROWS and KVD == PAGE_SIZE
    full = lambda a: pl.BlockSpec(a.shape, lambda b, pt: (0,) * a.ndim)
    return pl.pallas_call(
        functools.partial(_compress_pages_kernel, n_pages=n_pages), name="compress_pages",
        grid_spec=pltpu.PrefetchScalarGridSpec(
            num_scalar_prefetch=1, grid=(DB,),
            in_specs=[pl.BlockSpec(memory_space=pl.ANY), full(pos2), full(w1p), full(w2p)],
            out_specs=pl.BlockSpec((None, 2, 128, KVD), lambda b, pt: (b, 0, 0, 0)),
            scratch_shapes=[pltpu.VMEM((2, CMP_ROWS, KVD), F32), pltpu.VMEM((256 * BLOCK_PITCH, KVD), F32),
                            pltpu.VMEM((256, KVD), F32), pltpu.SemaphoreType.DMA((2,))]),
        out_shape=jax.ShapeDtypeStruct((DB, 2, 128, KVD), BF16),
        compiler_params=_cparams(("arbitrary",)),
    )(page_table.reshape(-1), pool_t, pos2, w1p, w2p)


def _topk_extract(v, k, want_idx, axis=1):
    pos = lax.broadcasted_iota(jnp.int32, v.shape, axis).astype(F32)
    idx = []
    for _ in range(k):
        mx = jnp.max(v, axis=axis, keepdims=True)
        first = jnp.min(jnp.where(v == mx, pos, 1e9), axis=axis, keepdims=True)
        v = jnp.where(pos == first, -jnp.inf, v)
        if want_idx:
            idx.append(first)
    return v, idx


def _cmp_kernel(q_ref, kc_ref, vc_ref, gates_ref, nb_ref, oc_ref, m_ref, *, nc):
    t0 = pl.program_id(1) * TQ
    half = nc // 2
    lane = lax.broadcasted_iota(jnp.int32, (nc, LANES), 1)
    crow = lax.broadcasted_iota(jnp.int32, (nc, LANES), 0)
    cnat = jnp.where(crow < half, 2 * crow, 2 * (crow - half) + 1)
    cbase = t0 // L_CMP - 8
    e = jnp.where(lane < 16, lane, lane - 16)
    ext = jnp.where((lane < 32) & (cnat == cbase + e), 1.0, 0.0)
    ext = jnp.where((lane == 32) & (cnat >= cbase + 16), NEG, ext).astype(BF16)
    lo = lane < HEAD_DIM
    kc = kc_ref[...]
    vc = vc_ref[...]
    zero = jnp.zeros_like(kc)
    rhs = [jnp.concatenate([jnp.where(lo, kc, zero), ext], axis=1),
           jnp.concatenate([jnp.where(lo, zero, kc), ext], axis=1)]
    vg = [jnp.where(lo, vc, zero), jnp.where(lo, zero, vc)]
    gates = gates_ref[...]
    imp = [jnp.zeros((TQ, nc), F32), jnp.zeros((TQ, nc), F32)]
    def scores(hh):
        qs = q_ref[:, (hh // 2) * LANES:(hh // 2 + 1) * LANES]
        return _dot_nt(jnp.concatenate([qs, nb_ref[hh]], axis=1), rhs[hh % 2])

    s_next = scores(0)
    osl = None
    for hh in range(N_HEADS):
        g = hh % 2
        s = s_next
        if hh + 1 < N_HEADS:
            s_next = scores(hh + 1)
        m = jnp.max(s, axis=-1, keepdims=True)
        ex = jnp.exp2(s - m)
        l = jnp.sum(ex, axis=-1, keepdims=True)
        p = ex * jnp.where(m > 0.1 * NEG, 1.0 / l, 0.0)
        imp[g] = imp[g] + p
        o = gates[:, hh:hh + 1] * _dot(p.astype(BF16), vg[g])
        if g == 0:
            osl = o
        else:
            oc_ref[:, (hh // 2) * LANES:(hh // 2 + 1) * LANES] = (osl + o).astype(BF16)
    blk = lax.broadcasted_iota(jnp.int32, (half, TQ), 0)
    tcol = t0 + lax.broadcasted_iota(jnp.int32, (half, TQ), 1)
    forced = (blk == tcol // L_SEL) | (blk == 0)
    started = blk * L_SEL <= tcol
    for g in range(N_KV):
        v = (imp[g][:, :half] + imp[g][:, half:]).T
        v = jnp.where(forced, FORCED_SCORE, jnp.where(started, v, -1.0))
        v, _ = _topk_extract(v, N_SEL, False, axis=0)
        m_ref[:, g * half:(g + 1) * half] = jnp.where(v == -jnp.inf, 0.0, NEG).T.astype(BF16)


def _cmp_call(q, kc, vc, gates, nbt):
    B, S, _ = q.shape
    nc = kc.shape[1]
    row = lambda w: pl.BlockSpec((None, TQ, w), lambda b, t: (b, t, 0))
    perb = lambda a: pl.BlockSpec((None,) + a.shape[1:], lambda b, t: (b,) + (0,) * (a.ndim - 1))
    full = lambda a: pl.BlockSpec(a.shape, lambda b, t: (0,) * a.ndim)
    return pl.pallas_call(
        functools.partial(_cmp_kernel, nc=nc), name="cmp", grid=(B, S // TQ),
        in_specs=[row(QD), perb(kc), perb(vc), row(LANES), full(nbt)],
        out_specs=[row(QD), row(nc)],
        out_shape=[jax.ShapeDtypeStruct((B, S, QD), BF16), jax.ShapeDtypeStruct((B, S, nc), BF16)],
        compiler_params=_cparams(("arbitrary", "arbitrary")),
    )(q, kc, vc, gates, nbt)


def _sel_kernel(q_ref, m_ref, gates_ref, k_ref, v_ref, oh_ref, bt_ref, os_ref, lhs_s, acc_s, m_s, *, nh):
    qt = pl.program_id(1)
    tph = HALF_KEYS // KT
    for j in range(GROUP):
        qs = q_ref[:, j * LANES:(j + 1) * LANES]
        for h in range(nh):
            for g in range(N_KV):
                c0 = (g * nh + h) * LANES
                lhs_s[(h * 2 + g) * GROUP + j] = jnp.concatenate([qs, m_ref[:, c0:c0 + LANES]], axis=1)
    m_s[...] = jnp.full(m_s.shape, NEG, F32)
    acc_s[...] = jnp.zeros(acc_s.shape, F32)
    lo_q = lax.broadcasted_iota(jnp.int32, (TQ, LANES), 1) < HEAD_DIM

    def step(kt, nk, kind):
        rows = nk * KT
        k_t = k_ref[pl.ds(pl.multiple_of(kt * KT, KT), rows), :]
        v_t = v_ref[pl.ds(pl.multiple_of(kt * KT, KT), rows), :]
        h = kt // tph
        oh = oh_ref[pl.ds(pl.multiple_of((kt - h * tph) * KT, KT), rows), :]
        lane = lax.broadcasted_iota(jnp.int32, (rows, LANES), 1)
        lo_k = lane < HEAD_DIM
        zero = jnp.zeros_like(k_t)
        rhs = [jnp.concatenate([jnp.where(lo_k, k_t, zero), oh], axis=1),
               jnp.concatenate([jnp.where(lo_k, zero, k_t), oh], axis=1)]
        vg = [jnp.where(lo_k, v_t, jnp.where(lane == HEAD_DIM, 1.0, 0.0).astype(BF16)),
              jnp.where(lo_k, jnp.where(lane == 0, 1.0, 0.0).astype(BF16), v_t)]
        def scores(hh):
            g = hh % 2
            return _dot_nt(lhs_s[(h * 2 + g) * GROUP + hh // 2], rhs[g])

        s_next = scores(0)
        for hh in range(N_HEADS):
            s = s_next
            if hh + 1 < N_HEADS:
                s_next = scores(hh + 1)
            if kind:
                near, diag = bt_ref[0, hh], bt_ref[1, hh]
                s00, s01, s10, s11 = s[:SUB, :SUB], s[:SUB, SUB:], s[SUB:, :SUB], s[SUB:, SUB:]
                if kind == 1:
                    top, bot = [s00, s01 + near], [s10, s11]
                else:
                    top, bot = [s00 + diag, jnp.full((SUB, SUB), NEG, F32)], [s10 + near, s11 + diag]
                s = jnp.concatenate([jnp.concatenate(top, axis=1), jnp.concatenate(bot, axis=1)], axis=0)
            mp = m_s[hh]
            mc = jnp.broadcast_to(jnp.max(s, axis=-1, keepdims=True), (TQ, LANES))
            mn = jnp.maximum(mp, mc)
            m_s[hh] = mn
            p = jnp.exp2(s - jnp.concatenate([mn] * (rows // LANES), axis=1))
            acc_s[hh] = acc_s[hh] * jnp.exp2(mp - mn) + _dot(p.astype(BF16), vg[hh % 2])

    step(qt, 1, 2)

    @pl.when(qt >= 1)
    def _():
        step(qt - 1, 1, 1)

    nfar = jnp.maximum(qt - 1, 0)
    nbig = nfar // FAR_TILES

    def big(i, c):
        step(i * FAR_TILES, FAR_TILES, False)
        return c

    def small(kt, c):
        step(kt, 1, False)
        return c

    lax.fori_loop(0, nbig, big, 0)
    lax.fori_loop(nbig * FAR_TILES, nfar, small, 0)

    gates = gates_ref[...]
    for j in range(GROUP):
        a0 = acc_s[2 * j]
        a1 = acc_s[2 * j + 1]
        c = N_HEADS + 2 * j
        sc0 = gates[:, c:c + 1] / a0[:, HEAD_DIM:HEAD_DIM + 1]
        sc1 = gates[:, c + 1:c + 2] / a1[:, 0:1]
        os_ref[:, j * LANES:(j + 1) * LANES] = jnp.where(lo_q, a0 * sc0, a1 * sc1).astype(BF16)


def _sel_call(q, msk, gates, kb, vb, onehot, btab):
    B, S, _ = q.shape
    nh = msk.shape[2] // (2 * LANES)
    row = lambda w: pl.BlockSpec((None, TQ, w), lambda b, t: (b, t, 0))
    perb = lambda a: pl.BlockSpec((None,) + a.shape[1:], lambda b, t: (b,) + (0,) * (a.ndim - 1))
    full = lambda a: pl.BlockSpec(a.shape, lambda b, t: (0,) * a.ndim)
    return pl.pallas_call(
        functools.partial(_sel_kernel, nh=nh), name="sel", grid=(B, S // TQ),
        in_specs=[row(QD), row(msk.shape[2]), row(LANES), perb(kb), perb(vb), full(onehot), full(btab)],
        out_specs=row(QD),
        out_shape=jax.ShapeDtypeStruct((B, S, QD), BF16),
        scratch_shapes=[pltpu.VMEM((nh * 2 * GROUP, TQ, 2 * LANES), BF16), pltpu.VMEM((N_HEADS, TQ, LANES), F32),
                        pltpu.VMEM((N_HEADS, TQ, LANES), F32)],
        compiler_params=_cparams(("arbitrary", "arbitrary")),
    )(q, msk, gates, kb, vb, onehot, btab)


def _win_kernel(q_ref, gates_ref, k0_ref, k1_ref, k2_ref, v0_ref, v1_ref, v2_ref, wt_ref, ow_ref):
    qt = pl.program_id(1)
    lo_k = lax.broadcasted_iota(jnp.int32, (3 * KT, LANES), 1) < HEAD_DIM
    kcat = jnp.concatenate([k0_ref[...], k1_ref[...], k2_ref[...]], axis=0)
    vcat = jnp.concatenate([v0_ref[...], v1_ref[...], v2_ref[...]], axis=0)
    lane_k = lax.broadcasted_iota(jnp.int32, (3 * KT, LANES), 1)
    zero = jnp.zeros_like(kcat)
    kg = [jnp.where(lo_k, kcat, zero), jnp.where(lo_k, zero, kcat)]
    vg = [jnp.where(lo_k, vcat, jnp.where(lane_k == HEAD_DIM, 1.0, 0.0).astype(BF16)),
          jnp.where(lo_k, jnp.where(lane_k == 0, 1.0, 0.0).astype(BF16), vcat)]
    lo_q = lax.broadcasted_iota(jnp.int32, (TQ, LANES), 1) < HEAD_DIM
    gates = gates_ref[...]
    ii = lax.broadcasted_iota(jnp.int32, (SUB, SUB), 0)
    jj = lax.broadcasted_iota(jnp.int32, (SUB, SUB), 1)
    far_edge = jnp.where(jj >= ii, 0.0, NEG)
    outside = jnp.full((SUB, SUB), NEG, F32)

    def scores(hh):
        return _dot_nt(q_ref[:, (hh // 2) * LANES:(hh // 2 + 1) * LANES], kg[hh % 2])

    def body(edge):
        if edge:
            lane = lax.broadcasted_iota(jnp.int32, (1, 3 * KT), 1)
            pen = jnp.where(lane < KT, jnp.where(qt >= 2, 0.0, NEG),
                            jnp.where(lane < 2 * KT, jnp.where(qt >= 1, 0.0, NEG), 0.0))
        s_next = scores(0)
        pv = [None, None]
        for hh in range(N_HEADS):
            j, g = hh // 2, hh % 2
            s = s_next
            if hh + 1 < N_HEADS:
                s_next = scores(hh + 1)
            near, diag = wt_ref[0, hh], wt_ref[1, hh]
            col = lambda a, c: s[a * SUB:(a + 1) * SUB, c * SUB:(c + 1) * SUB]
            top = [col(0, 0) + far_edge, col(0, 1), col(0, 2), col(0, 3) + near, col(0, 4) + diag, outside]
            bot = [outside, col(1, 1) + far_edge, col(1, 2), col(1, 3), col(1, 4) + near, col(1, 5) + diag]
            s = jnp.concatenate([jnp.concatenate(top, axis=1), jnp.concatenate(bot, axis=1)], axis=0)
            if edge:
                s = s + pen
            ex = jnp.exp2(s - jnp.max(s, axis=-1, keepdims=True))
            pv[g] = _dot(ex.astype(BF16), vg[g])
            if g == 1:
                c = 2 * N_HEADS + 2 * j
                sc0 = gates[:, c:c + 1] / pv[0][:, HEAD_DIM:HEAD_DIM + 1]
                sc1 = gates[:, c + 1:c + 2] / pv[1][:, 0:1]
                ow_ref[:, j * LANES:(j + 1) * LANES] = jnp.where(lo_q, pv[0] * sc0, pv[1] * sc1).astype(BF16)

    @pl.when(qt >= 2)
    def _():
        body(False)

    @pl.when(qt < 2)
    def _():
        body(True)


def _win_call(q, gates, kb, vb, wtab):
    B, S, _ = q.shape
    row = lambda w: pl.BlockSpec((None, TQ, w), lambda b, t: (b, t, 0))
    back = lambda d: pl.BlockSpec((None, KT, KVD), lambda b, t: (b, jnp.maximum(t - d, 0), 0))
    full = lambda a: pl.BlockSpec(a.shape, lambda b, t: (0,) * a.ndim)
    return pl.pallas_call(
        _win_kernel, name="win", grid=(B, S // TQ),
        in_specs=[row(QD), row(LANES), back(2), back(1), back(0), back(2), back(1), back(0), full(wtab)],
        out_specs=row(QD),
        out_shape=jax.ShapeDtypeStruct((B, S, QD), BF16),
        compiler_params=_cparams(("arbitrary", "arbitrary")),
    )(q, gates, kb, kb, kb, vb, vb, vb, wtab)


def _merge_kernel(za_ref, oc_ref, os_ref, ow_ref, ga_ref, gb_ref, x_ref, wuc_ref, wua_ref, wo_ref, g_ref, b_ref,
                  h_ref):
    a = _dot(za_ref[...], wuc_ref[...])
    ob = oc_ref[...].astype(F32) + os_ref[...].astype(F32) + ow_ref[...].astype(F32)
    bm = _dot(ob.astype(BF16), wua_ref[...])
    pre = ga_ref[...].astype(F32) * a + gb_ref[...].astype(F32) * bm
    mix = _dot(pre.astype(BF16), wo_ref[...])
    h_ref[...] = _layer_norm(ALPHA * x_ref[...] + mix, g_ref[...], b_ref[...])


def _merge_call(za, oc, os_, ow, ga, gb, x, wuc, wua, wo, g, b, tm):
    R = x.shape[0]
    row = lambda w: pl.BlockSpec((tm, w), lambda i: (i, 0))
    full = lambda a: pl.BlockSpec(a.shape, lambda i: (0,) * a.ndim)
    return pl.pallas_call(
        _merge_kernel, name="merge", grid=(R // tm,),
        in_specs=[row(D_CONV), row(QD), row(QD), row(QD), row(D_MODEL), row(D_MODEL), row(D_MODEL),
                  full(wuc), full(wua), full(wo), full(g), full(b)],
        out_specs=row(D_MODEL), out_shape=jax.ShapeDtypeStruct((R, D_MODEL), F32),
        compiler_params=_cparams(("arbitrary",)),
    )(za, oc, os_, ow, ga, gb, x, wuc, wua, wo, g, b)


def _ffn_kernel(h_ref, wg_ref, wu_ref, wd_ref, g_ref, b_ref, y_ref):
    h = h_ref[...]
    hb = h.astype(BF16)
    act = jax.nn.silu(_dot(hb, wg_ref[...])) * _dot(hb, wu_ref[...])
    f = _dot(act.astype(BF16), wd_ref[...])
    y_ref[...] = _layer_norm(ALPHA * h + f, g_ref[...], b_ref[...])


def _ffn_call(h, wg, wu, wd, g, b, tm):
    R = h.shape[0]
    row = lambda w: pl.BlockSpec((tm, w), lambda i: (i, 0))
    full = lambda a: pl.BlockSpec(a.shape, lambda i: (0,) * a.ndim)
    return pl.pallas_call(
        _ffn_kernel, name="ffn", grid=(R // tm,),
        in_specs=[row(D_MODEL), full(wg), full(wu), full(wd), full(g), full(b)],
        out_specs=row(D_MODEL), out_shape=jax.ShapeDtypeStruct((R, D_MODEL), F32),
        compiler_params=_cparams(("arbitrary",)),
    )(h, wg, wu, wd, g, b)


def _scmp_kernel(qh_ref, kc_ref, vc_ref, gate_ref, bc_ref, oc_ref, imp_ref, *, nc):
    half = nc // 2
    s = _dot_nt(qh_ref[...], kc_ref[...]) + bc_ref[...]
    m = jnp.max(s, axis=-1, keepdims=True)
    ex = jnp.exp2(s - m)
    p = ex * (1.0 / jnp.sum(ex, axis=-1, keepdims=True))
    o = _dot(p.astype(BF16), vc_ref[...]) * gate_ref[...]
    oc_ref[...] = o[0:GROUP] + o[GROUP:2 * GROUP]
    for g in range(N_KV):
        pg = jnp.sum(p[g * GROUP:(g + 1) * GROUP], axis=0, keepdims=True)
        imp_ref[g:g + 1, :] = pg[:, :half] + pg[:, half:]


def _scmp_call(qh, kc, vc, gate_c, bc):
    DB, _, nc, _ = (kc.shape[0], None, kc.shape[1], None)
    perb = lambda a: pl.BlockSpec((None,) + a.shape[1:], lambda b: (b,) + (0,) * (a.ndim - 1))
    full = lambda a: pl.BlockSpec(a.shape, lambda b: (0,) * a.ndim)
    return pl.pallas_call(
        functools.partial(_scmp_kernel, nc=nc), name="sample_cmp", grid=(DB,),
        in_specs=[perb(qh), perb(kc), perb(vc), perb(gate_c), full(bc)],
        out_specs=[pl.BlockSpec((None, GROUP, LANES), lambda b: (b, 0, 0)),
                   pl.BlockSpec((None, N_KV, nc // 2), lambda b: (b, 0, 0))],
        out_shape=[jax.ShapeDtypeStruct((DB, GROUP, LANES), F32), jax.ShapeDtypeStruct((DB, N_KV, nc // 2), F32)],
        compiler_params=_cparams(("arbitrary",)),
    )(qh, kc, vc, gate_c, bc)


def _stopk_kernel(imp_ref, idx_ref, *, k):
    v = imp_ref[...]
    lane = lax.broadcasted_iota(jnp.int32, v.shape, 1)
    v = jnp.where(lane == 0, FORCED_SCORE, v)
    _, idx = _topk_extract(v, k, True)
    out = jnp.zeros(v.shape, F32)
    for i, col in enumerate(idx):
        out = jnp.where(lane == i, col, out)
    idx_ref[...] = out.astype(jnp.int32)


def _stopk_call(imp, k):
    return pl.pallas_call(
        functools.partial(_stopk_kernel, k=k), name="sample_topk",
        out_shape=jax.ShapeDtypeStruct(imp.shape, jnp.int32),
        compiler_params=pltpu.CompilerParams(vmem_limit_bytes=VMEM_LIMIT),
    )(imp)


def _ssel_kernel(idx_ref, pt_ref, q_ref, knew_ref, vnew_ref, kwnew_ref, vwnew_ref, kwcol_ref, vwcol_ref,
                 gate_s_ref, gate_w_ref, kwin_ref, vwin_ref, tn_ref, tw_ref, b0_ref, kpool_ref, vpool_ref,
                 os_ref, ow_ref, kwo_ref, vwo_ref, kbuf, vbuf, sem, *, n_pages, nsel, n_last):
    b = pl.program_id(0)
    nb = pl.num_programs(0)
    slot = lax.rem(b, 2)

    def copies(bb, sl, g, s):
        n = idx_ref[(bb * N_KV + g) * nsel + s]
        page = pt_ref[bb * n_pages + n // 2]
        dst = pl.ds(s * PAGE_SIZE, PAGE_SIZE)
        return (pltpu.make_async_copy(kpool_ref.at[page, g], kbuf.at[sl, g, :, dst], sem.at[sl, 0]),
                pltpu.make_async_copy(vpool_ref.at[page, g], vbuf.at[sl, g, :, dst], sem.at[sl, 1]))

    def start_all(bb, sl):
        for g in range(N_KV):
            for s in range(nsel):
                ck, cv = copies(bb, sl, g, s)
                ck.start()
                cv.start()

    @pl.when(b == 0)
    def _():
        start_all(b, 0)

    @pl.when(b + 1 < nb)
    def _():
        start_all(b + 1, 1 - slot)

    def attend(qg, s, k_new, v_new, vt, bias_new, gate):
        s_new = jnp.sum(qg.astype(F32) * k_new.astype(BF16).astype(F32), axis=-1, keepdims=True) + bias_new
        m = jnp.maximum(jnp.max(s, axis=-1, keepdims=True), s_new)
        ex = jnp.exp2(s - m)
        ex_new = jnp.exp2(s_new - m)
        l = jnp.sum(ex, axis=-1, keepdims=True) + ex_new
        o = _dot_nt(ex.astype(BF16), vt.astype(BF16)) + ex_new.astype(BF16).astype(F32) * v_new.astype(BF16).astype(F32)
        return o * (gate / l)

    wb = kwin_ref.shape[-1]
    lane_w = lax.broadcasted_iota(jnp.int32, (HEAD_DIM, wb), 1)
    for g in range(N_KV):
        qg = q_ref[g]
        kw = kwin_ref[g]
        vw = vwin_ref[g]
        s = _dot(qg, kw.astype(BF16)) + tw_ref[g]
        ow_ref[g] = attend(qg, s, kwnew_ref[g], vwnew_ref[g], vw, b0_ref[g], gate_w_ref[g])
        kwo_ref[g] = jnp.where(lane_w == wb - 1, kwcol_ref[g], pltpu.roll(kw, wb - 1, 1))
        vwo_ref[g] = jnp.where(lane_w == wb - 1, vwcol_ref[g], pltpu.roll(vw, wb - 1, 1))

    for g in range(N_KV):
        for s in range(nsel):
            ck, cv = copies(b, slot, g, s)
            ck.wait()
            cv.wait()

    upper = lax.broadcasted_iota(jnp.int32, (GROUP, PAGE_SIZE), 1) >= L_SEL
    keep_upper = jnp.where(upper, 0.0, NEG)
    keep_lower = jnp.where(upper, NEG, 0.0)
    for g in range(N_KV):
        qg = q_ref[g]
        pieces = []
        for s in range(nsel):
            n = idx_ref[(b * N_KV + g) * nsel + s]
            piece = jnp.where(n % 2 == 1, keep_upper, keep_lower)
            piece = piece + jnp.where(n == n_last, 1.0, 0.0) * tn_ref[0, g]
            piece = piece + jnp.where(n == n_last - 1, 1.0, 0.0) * tn_ref[1, g]
            pieces.append(piece)
        s = _dot(qg, kbuf[slot, g].astype(BF16)) + jnp.concatenate(pieces, axis=1)
        os_ref[g] = attend(qg, s, knew_ref[g], vnew_ref[g], vbuf[slot, g], b0_ref[g], gate_s_ref[g])


def _ssel_call(idx, page_table, qg, knew, vnew, kwnew, vwnew, kwcol, vwcol, gate_s, gate_w, kwin, vwin, tn, tw, b0,
               kpool, vpool):
    DB, n_pages = page_table.shape
    nsel = idx.shape[-1]
    wb = kwin.shape[-1]
    n_last = n_pages * (PAGE_SIZE // L_SEL) - 1
    perb = lambda a: pl.BlockSpec((None,) + a.shape[1:], lambda b, *_: (b,) + (0,) * (a.ndim - 1))
    full = lambda a: pl.BlockSpec(a.shape, lambda b, *_: (0,) * a.ndim)
    anyspec = pl.BlockSpec(memory_space=pl.ANY)
    head_o = pl.BlockSpec((None, N_KV, GROUP, HEAD_DIM), lambda b, *_: (b, 0, 0, 0))
    win_o = pl.BlockSpec((None, N_KV, HEAD_DIM, wb), lambda b, *_: (b, 0, 0, 0))
    head_s = jax.ShapeDtypeStruct((DB, N_KV, GROUP, HEAD_DIM), F32)
    win_s = jax.ShapeDtypeStruct((DB, N_KV, HEAD_DIM, wb), F32)
    return pl.pallas_call(
        functools.partial(_ssel_kernel, n_pages=n_pages, nsel=nsel, n_last=n_last), name="sample_sel_win",
        grid_spec=pltpu.PrefetchScalarGridSpec(
            num_scalar_prefetch=2, grid=(DB,),
            in_specs=[perb(qg), perb(knew), perb(vnew), perb(kwnew), perb(vwnew), perb(kwcol), perb(vwcol),
                      perb(gate_s), perb(gate_w), perb(kwin), perb(vwin), full(tn), full(tw), full(b0),
                      anyspec, anyspec],
            out_specs=[head_o, head_o, win_o, win_o],
            scratch_shapes=[pltpu.VMEM((2, N_KV, HEAD_DIM, nsel * PAGE_SIZE), F32),
                            pltpu.VMEM((2, N_KV, HEAD_DIM, nsel * PAGE_SIZE), F32),
                            pltpu.SemaphoreType.DMA((2, 2))]),
        out_shape=[head_s, head_s, win_s, win_s],
        compiler_params=_cparams(("arbitrary",)),
    )(idx.reshape(-1), page_table.reshape(-1), qg, knew, vnew, kwnew, vwnew, kwcol, vwcol, gate_s, gate_w,
      kwin, vwin, tn, tw, b0, kpool, vpool)


def _perm_w_in(w_in):
    o_q = 3 * D_CONV
    o_kv = o_q + QD
    o_ng = o_kv + 6 * KVD
    o_mg = o_ng + 3 * N_HEADS
    q = w_in[:, o_q:o_kv].reshape(D_MODEL, N_KV, GROUP, HEAD_DIM).transpose(0, 2, 1, 3).reshape(D_MODEL, QD)
    q = q * (HEAD_DIM ** -0.5 * LOG2E)
    ng = w_in[:, o_ng:o_mg].reshape(D_MODEL, N_KV, GROUP, 3).transpose(0, 3, 2, 1).reshape(D_MODEL, 3 * N_HEADS)
    ng = jnp.pad(ng, ((0, 0), (0, LANES - 3 * N_HEADS)))
    return jnp.concatenate([w_in[:, :o_q], q, w_in[:, o_kv:o_ng], ng, w_in[:, o_mg:]], axis=1).astype(BF16)


def _perm_heads(t):
    return t.reshape(t.shape[:-1] + (N_KV, GROUP)).swapaxes(-1, -2).reshape(t.shape)


def _cmp_weights(w1, w2, pos):
    eye = jnp.eye(N_KV, dtype=F32)
    w1p = jnp.einsum('ldh,ab->ladbh', w1.reshape(L_CMP, HEAD_DIM, CMP_HID), eye)
    w1p = w1p.reshape(L_CMP // 2, 2 * KVD, N_KV * CMP_HID).astype(BF16)
    w2p = jnp.einsum('hd,ab->ahbd', w2, eye).reshape(N_KV * CMP_HID, KVD).astype(BF16)
    pos2 = jnp.tile(pos[:, None, :], (1, N_KV, 1)).reshape(L_CMP, 1, KVD)
    return pos2, w1p, w2p


def kernel(x_prompt, x_sample, cache_k_cmp, cache_v_cmp, cache_k_slc, cache_v_slc, state_k_win, state_v_win,
           state_conv, page_table, w_in, conv_w, w_cmp_k1, w_cmp_k2, pos_cmp_k, w_cmp_v1, w_cmp_v2, pos_cmp_v,
           rel_bias, w_up_conv, w_up_attn, w_o, ln1_g, ln1_b, w_gate_ffn, w_up_ffn, w_down_ffn, ln2_g, ln2_b):
    B, S, _ = x_prompt.shape
    DB = x_sample.shape[0]
    n_pages = page_table.shape[1]
    P = n_pages * PAGE_SIZE
    n_pool = cache_k_cmp.shape[1]
    wb = state_k_win.shape[2]
    assert x_sample.shape[1] == 1 and w_in.shape[0] == 1
    assert S % HALF_KEYS == 0 and P == HALF_KEYS and wb == WINDOW and S >= WINDOW
    nc = S // L_CMP
    nh = S // HALF_KEYS

    w_perm = _perm_w_in(w_in[0])
    cw = conv_w[0]
    cmp_k = _cmp_weights(w_cmp_k1[0], w_cmp_k2[0], pos_cmp_k[0])
    cmp_v = _cmp_weights(w_cmp_v1[0], w_cmp_v2[0], pos_cmp_v[0])
    wuc = w_up_conv[0].astype(BF16)
    wua = w_up_attn[0].reshape(N_KV, GROUP, HEAD_DIM, D_MODEL).transpose(1, 0, 2, 3).reshape(QD, D_MODEL).astype(BF16)
    wo = w_o[0].astype(BF16)
    wg, wu, wd = w_gate_ffn[0].astype(BF16), w_up_ffn[0].astype(BF16), w_down_ffn[0].astype(BF16)
    g1, b1, g2, b2 = ln1_g, ln1_b, ln2_g, ln2_b

    bucket = _bucket_table()
    bias_p = (rel_bias[bucket] - rel_bias[N_BUCKETS - 1][None, :]) * LOG2E
    bias_pp = _perm_heads(bias_p)
    ii = np.arange(TQ)[:, None]
    dist_c = ii - L_CMP * np.arange(16)[None, :] + 8 * L_CMP - (L_CMP - 1)
    nb = _bias_from_dist(bias_pp, dist_c, True)
    nb_hi = nb.astype(BF16)
    nb_lo = jnp.where(jnp.asarray(dist_c < 0), 0.0, nb - nb_hi.astype(F32)).astype(BF16)
    nbt = jnp.concatenate([nb_hi, nb_lo, jnp.ones((N_HEADS, TQ, 1), BF16),
                           jnp.zeros((N_HEADS, TQ, LANES - 33), BF16)], axis=-1)
    wtab =jnp.stack([_toeplitz(bias_pp, SUB, False, SUB), _toeplitz(bias_pp, 0, True, SUB)])
    onehot = jnp.asarray((np.arange(HALF_KEYS)[:, None] // L_SEL == np.arange(LANES)[None, :]), BF16)

    (za, q, kcm, vcm, ksl, vsl, kwn, vwn, kslb, vslb, kwnb, vwnb, gates, ga, gb, conv_p) = _proj_call(
        x_prompt, w_perm, cw, None, 256)
    kc = _compress_prompt(kcm, *cmp_k).reshape(B, nc, KVD)
    vc = _compress_prompt(vcm, *cmp_v).reshape(B, nc, KVD)
    oc, msk = _cmp_call(q, kc, vc, gates, nbt)
    os_ = _sel_call(q, msk, gates, kslb, vslb, onehot, wtab)
    ow = _win_call(q, gates, kwnb, vwnb, wtab)
    f2 = lambda t: t.reshape(B * S, t.shape[-1])
    h = _merge_call(f2(za), f2(oc), f2(os_), f2(ow), f2(ga), f2(gb), f2(x_prompt), wuc, wua, wo, g1, b1, 256)
    y_p = _ffn_call(h, wg, wu, wd, g2, b2, 256).reshape(B, S, D_MODEL)
    kv5 = lambda t: t.reshape(1, B, S, N_KV, HEAD_DIM)
    keep = min(WINDOW, S)
    prompt_state = (kv5(kcm), kv5(vcm), kv5(ksl), kv5(vsl), kv5(kwn)[:, :, S - keep:], kv5(vwn)[:, :, S - keep:],
                    conv_p[None])

    xs = x_sample.reshape(1, DB, D_MODEL)
    p0 = state_conv[0, :, 0, :][None]
    p1 = state_conv[0, :, 1, :][None]
    (za_s, q_s, kcm_s, vcm_s, ksl_s, vsl_s, kwn_s, vwn_s, gates_s, ga_s, gb_s, u_s) = _proj_call(
        xs, w_perm, cw, (p0, p1), DB)
    pool_rows = lambda c: jnp.transpose(c[0], (0, 2, 3, 1)).reshape(n_pool, KVD, PAGE_SIZE)
    kc_s = _compress_pages(page_table, pool_rows(cache_k_cmp), *cmp_k).reshape(DB, P // L_CMP, KVD)
    vc_s = _compress_pages(page_table, pool_rows(cache_v_cmp), *cmp_v).reshape(DB, P // L_CMP, KVD)
    lane_lo = jnp.arange(LANES) < HEAD_DIM
    q3 = q_s.reshape(DB, GROUP, LANES)
    qh = jnp.concatenate([jnp.where(lane_lo, q3, 0), jnp.where(lane_lo, 0, q3)], axis=1)
    gsm = gates_s.reshape(DB, LANES)[:, :3 * N_HEADS].reshape(DB, 3, GROUP, N_KV)
    gsm = gsm.transpose(1, 0, 3, 2).reshape(3, DB, N_HEADS, 1)
    row_lo = (jnp.arange(N_HEADS) < GROUP)[:, None]
    lane_own = jnp.where(row_lo, lane_lo[None, :], ~lane_lo[None, :])
    gate3 = jnp.where(lane_own, gsm, 0.0)
    ncs = P // L_CMP
    cnat = np.concatenate([np.arange(0, ncs, 2), np.arange(1, ncs, 2)])
    bc = _bias_from_dist(bias_p, P - (L_CMP * cnat + L_CMP - 1), False)
    oc_s, imp_s = _scmp_call(qh, kc_s, vc_s, gate3[0], bc)
    nsel = N_SEL - 1
    idx = _stopk_call(imp_s.reshape(DB * N_KV, ncs // 2), nsel)[:, :nsel]
    hg = lambda t: t.reshape((N_KV, GROUP) + t.shape[1:])
    zpad = jnp.zeros((N_HEADS, L_SEL), F32)
    t_last = _bias_from_dist(bias_p, L_SEL - np.arange(L_SEL), False)
    t_prev = _bias_from_dist(bias_p, 2 * L_SEL - np.arange(L_SEL), False)
    tn = jnp.stack([hg(jnp.concatenate([zpad, t_last], 1)), hg(jnp.concatenate([t_prev, zpad], 1))])
    tw = hg(_bias_from_dist(bias_p, wb - np.arange(wb), False))
    b0 = hg(bias_p[0][:, None])
    qg = q_s.reshape(DB, GROUP, N_KV, HEAD_DIM).transpose(0, 2, 1, 3)
    gate_g = gsm.reshape(3, DB, N_KV, GROUP, 1)
    rowv = lambda t: t.reshape(DB, N_KV, 1, HEAD_DIM)
    colv = lambda t: t.reshape(DB, N_KV, HEAD_DIM, 1)
    tpose = lambda t: jnp.transpose(t[0], (0, 2, 3, 1))
    os_s, ow_s, kw_out, vw_out = _ssel_call(
        idx, page_table, qg, rowv(ksl_s), rowv(vsl_s), rowv(kwn_s), rowv(vwn_s), colv(kwn_s), colv(vwn_s),
        gate_g[1], gate_g[2], tpose(state_k_win), tpose(state_v_win), tn, tw, b0,
        tpose(cache_k_slc), tpose(cache_v_slc))
    fl = lambda t: t.reshape(DB, -1)
    slots = lambda t: t.transpose(0, 2, 1, 3).reshape(DB, QD)
    h_s = _merge_call(fl(za_s), fl(oc_s), slots(os_s), slots(ow_s), fl(ga_s), fl(gb_s), fl(xs), wuc, wua, wo,
                      g1, b1, DB)
    y_s = _ffn_call(h_s, wg, wu, wd, g2, b2, DB).reshape(DB, 1, D_MODEL)
    kv5s = lambda t: t.reshape(1, DB, 1, N_KV, HEAD_DIM)
    conv_s = jnp.stack([state_conv[0, :, 1, :], u_s.reshape(DB, D_CONV)], axis=1)[None]
    untp = lambda t: jnp.transpose(t, (0, 3, 1, 2))[None]
    sample_state = (kv5s(kcm_s), kv5s(vcm_s), kv5s(ksl_s), kv5s(vsl_s), untp(kw_out), untp(vw_out), conv_s)
    return (y_p, y_s) + prompt_state + sample_state
```

```python
import functools
import math

import numpy as np
import jax
import jax.numpy as jnp
from jax import lax
from jax.experimental import pallas as pl
from jax.experimental.pallas import tpu as pltpu

F32 = jnp.float32
BF16 = jnp.bfloat16

D_MODEL = 1024
D_CONV = 512
CONV_W = 3
N_HEADS = 16
HEAD_DIM = 64
N_KV = 2
GROUP = N_HEADS // N_KV
QD = N_HEADS * HEAD_DIM
KVD = N_KV * HEAD_DIM
L_CMP = 32
L_SEL = 64
N_SEL = 16
WINDOW = 512
CMP_HID = 256
FORCED_SCORE = 1e9
N_BUCKETS = 32
MAX_DIST = 128
D_FF = -(-8 * D_MODEL // (3 * 256)) * 256
DEPTH = 1
ALPHA = (2 * DEPTH) ** 0.25
LN_EPS = 1e-5
NEG = -1e30
PAGE_SIZE = 128
LOG2E = math.log2(math.e)

LANES = 128
VMEM_LIMIT = 56 * 1024 * 1024

C_CB, C_CC, C_CH, C_Q, C_KV, C_NG, C_MG, C_END = 0, 512, 1024, 1536, 2560, 3328, 3456, 5504

TQ = 256
KT = 256
HALF_KEYS = 8192
FAR_TILES = 4
SUB = 128


def _dot(a, b):
    return jnp.dot(a, b, preferred_element_type=F32)


def _dot_nt(a, b):
    return lax.dot_general(a, b, (((1,), (1,)), ((), ())), preferred_element_type=F32)


def _cparams(sem):
    return pltpu.CompilerParams(dimension_semantics=sem, vmem_limit_bytes=VMEM_LIMIT)


def _layer_norm(x, g, b):
    mu = jnp.mean(x, axis=-1, keepdims=True)
    xc = x - mu
    var = jnp.mean(xc * xc, axis=-1, keepdims=True)
    return xc * lax.rsqrt(var + LN_EPS) * g + b


def _bucket_table():
    d = np.arange(0, MAX_DIST + 1)
    exact = N_BUCKETS // 2
    log_ratio = (np.log(np.maximum(d, 1).astype(np.float32) / np.float32(exact))
                 / np.float32(math.log(MAX_DIST / exact))).astype(np.float32)
    large = np.minimum(exact + (log_ratio * np.float32(N_BUCKETS - exact)).astype(np.int32), N_BUCKETS - 1)
    return np.where(d < exact, d, large).astype(np.int32)


def _bias_from_dist(bias_p, dist, causal):
    dc = np.clip(dist, 0, MAX_DIST)
    val = jnp.moveaxis(bias_p[dc], -1, 0)
    if causal:
        val = jnp.where(jnp.asarray(dist < 0), NEG, val)
    return val


def _toeplitz(bias_p, off, causal, n=TQ):
    w = _bias_from_dist(bias_p, n - 1 - np.arange(2 * n - 1) + off, causal)
    w = jnp.pad(w, ((0, 0), (0, 1)))
    rows = jnp.tile(w, (1, n))[:, :n * (2 * n - 1)].reshape(w.shape[0], n, 2 * n - 1)
    return rows[:, :, n - 1:]


def _proj_kernel(*refs, tm, sample):
    if sample:
        (x_ref, w_ref, cw_ref, p0_ref, p1_ref, za_ref, q_ref, kcm_ref, vcm_ref, ksl_ref, vsl_ref, kwn_ref,
         vwn_ref, gates_ref, ga_ref, gb_ref, u_ref) = refs
    else:
        (x_ref, w_ref, cw_ref, za_ref, q_ref, kcm_ref, vcm_ref, ksl_ref, vsl_ref, kwn_ref, vwn_ref,
         kslb_ref, vslb_ref, kwnb_ref, vwnb_ref, gates_ref, ga_ref, gb_ref, conv_ref, ubuf) = refs
    xb = x_ref[...].astype(BF16)

    def mm(c0, c1):
        return _dot(xb, w_ref[:, c0:c1])

    u = mm(C_CB, C_CC) * mm(C_CH, C_Q)
    cc = mm(C_CC, C_CH)
    cw = cw_ref[...]
    if sample:
        u1 = p1_ref[...]
        u2 = p0_ref[...]
        u_ref[...] = u
    else:
        @pl.when(pl.program_id(1) == 0)
        def _():
            ubuf[0:8, :] = jnp.zeros((8, D_CONV), F32)
        ubuf[8:tm + 8, :] = u
        u1 = ubuf[7:tm + 7, :]
        u2 = ubuf[6:tm + 6, :]
        conv_ref[...] = ubuf[tm + 6:tm + 8, :]
        ubuf[0:8, :] = ubuf[tm:tm + 8, :]
    za_ref[...] = (cc * (cw[0:1] * u2 + cw[1:2] * u1 + cw[2:3] * u)).astype(BF16)
    q_ref[...] = mm(C_Q, C_KV).astype(BF16)
    kv = mm(C_KV, C_NG)
    for i, r in enumerate((kcm_ref, vcm_ref, ksl_ref, vsl_ref, kwn_ref, vwn_ref)):
        r[...] = kv[:, i * KVD:(i + 1) * KVD]
    if not sample:
        for i, r in enumerate((kslb_ref, vslb_ref, kwnb_ref, vwnb_ref)):
            r[...] = kv[:, (i + 2) * KVD:(i + 3) * KVD].astype(BF16)
    gates_ref[...] = jax.nn.sigmoid(mm(C_NG, C_MG))
    mg = jax.nn.sigmoid(mm(C_MG, C_END))
    ga_ref[...] = mg[:, :D_MODEL].astype(BF16)
    gb_ref[...] = mg[:, D_MODEL:].astype(BF16)


def _proj_call(x, w_perm, conv_w, prefix, tm):
    B, T, _ = x.shape
    sample = prefix is not None
    nt = T // tm
    row = lambda w: pl.BlockSpec((None, tm, w), lambda b, s: (b, s, 0))
    full = lambda a: pl.BlockSpec(a.shape, lambda b, s: (0,) * a.ndim)
    sds = lambda w, dt: jax.ShapeDtypeStruct((B, T, w), dt)
    in_specs = [row(D_MODEL), full(w_perm), full(conv_w)]
    args = [x, w_perm, conv_w]
    out_shape = [sds(D_CONV, BF16), sds(QD, BF16)] + [sds(KVD, F32)] * 6
    out_specs = [row(D_CONV), row(QD)] + [row(KVD)] * 6
    scratch = []
    if sample:
        in_specs += [row(D_CONV), row(D_CONV)]
        args += list(prefix)
    else:
        out_shape += [sds(KVD, BF16)] * 4
        out_specs += [row(KVD)] * 4
    out_shape += [sds(LANES, F32), sds(D_MODEL, BF16), sds(D_MODEL, BF16)]
    out_specs += [row(LANES), row(D_MODEL), row(D_MODEL)]
    if sample:
        out_shape.append(sds(D_CONV, F32))
        out_specs.append(row(D_CONV))
    else:
        out_shape.append(jax.ShapeDtypeStruct((B, CONV_W - 1, D_CONV), F32))
        out_specs.append(pl.BlockSpec((None, CONV_W - 1, D_CONV), lambda b, s: (b, 0, 0)))
        scratch.append(pltpu.VMEM((tm + 8, D_CONV), F32))
    return pl.pallas_call(
        functools.partial(_proj_kernel, tm=tm, sample=sample), name="proj",
        grid=(B, nt), in_specs=in_specs, out_specs=out_specs, out_shape=out_shape,
        scratch_shapes=scratch, compiler_params=_cparams(("arbitrary", "arbitrary")),
    )(*args)


CMP_ROWS = 256 * L_CMP


BLOCK_PITCH = 40


def _compress_rows(x_ref, pos_ref, w1_ref, w2_ref, tmp_ref, pitch):
    acc = None
    for lp in range(L_CMP // 2):
        xa = x_ref[pl.ds(2 * lp, 256, stride=pitch), :] + pos_ref[2 * lp]
        xb = x_ref[pl.ds(2 * lp + 1, 256, stride=pitch), :] + pos_ref[2 * lp + 1]
        t = _dot(jnp.concatenate([xa, xb], axis=1).astype(BF16), w1_ref[lp])
        acc = t if acc is None else acc + t
    hid = jax.nn.gelu(acc)
    tmp_ref[...] = _dot(hid.astype(BF16), w2_ref[...])
    return tmp_ref[pl.ds(0, 128, stride=2), :], tmp_ref[pl.ds(1, 128, stride=2), :]


def _compress_kernel(x_ref, pos_ref, w1_ref, w2_ref, o_ref, tmp_ref):
    ev, od = _compress_rows(x_ref, pos_ref, w1_ref, w2_ref, tmp_ref, L_CMP)
    o_ref[0] = ev.astype(BF16)
    o_ref[1] = od.astype(BF16)


def _compress_prompt(rows, pos2, w1p, w2p):
    B, S, _ = rows.shape
    nchunk = S // CMP_ROWS
    full = lambda a: pl.BlockSpec(a.shape, lambda b, i: (0,) * a.ndim)
    return pl.pallas_call(
        _compress_kernel, name="compress", grid=(B, nchunk),
        in_specs=[pl.BlockSpec((None, CMP_ROWS, KVD), lambda b, i: (b, i, 0)), full(pos2), full(w1p), full(w2p)],
        out_specs=pl.BlockSpec((None, 2, 128, KVD), lambda b, i: (b, 0, i, 0)),
        out_shape=jax.ShapeDtypeStruct((B, 2, nchunk * 128, KVD), BF16),
        scratch_shapes=[pltpu.VMEM((256, KVD), F32)],
        compiler_params=_cparams(("arbitrary", "arbitrary")),
    )(rows, pos2, w1p, w2p)


def _compress_pages_kernel(pt_ref, pool_ref, pos_ref, w1_ref, w2_ref, o_ref, buf, nat, tmp_ref, sem, *, n_pages):
    b = pl.program_id(0)
    nb = pl.num_programs(0)

    def copy(bb, p, slot):
        page = pt_ref[bb * n_pages + p]
        return pltpu.make_async_copy(pool_ref.at[page], buf.at[slot, pl.ds(PAGE_SIZE * p, PAGE_SIZE), :],
                                     sem.at[slot])

    def start_all(bb, slot):
        for p in range(n_pages):
            copy(bb, p, slot).start()

    slot = lax.rem(b, 2)

    @pl.when(b == 0)
    def _():
        start_all(b, 0)

    for p in range(n_pages):
        copy(b, p, slot).wait()

    @pl.when(b + 1 < nb)
    def _():
        start_all(b + 1, 1 - slot)

    per_page = PAGE_SIZE // L_CMP
    for p in range(n_pages):
        t = buf[slot, pl.ds(PAGE_SIZE * p, PAGE_SIZE), :].T
        for n in range(per_page):
            nat[pl.ds((per_page * p + n) * BLOCK_PITCH, L_CMP), :] = t[n * L_CMP:(n + 1) * L_CMP, :]
    ev, od = _compress_rows(nat, pos_ref, w1_ref, w2_ref, tmp_ref, BLOCK_PITCH)
    o_ref[0] = ev.astype(BF16)
    o_ref[1] = od.astype(BF16)


def _compress_pages(page_table, pool_t, pos2, w1p, w2p):
    DB, n_pages = page_table.shape
    assert n_pages * PAGE_SIZE == CMP_ROWS and KVD == PAGE_SIZE
    full = lambda a: pl.BlockSpec(a.shape, lambda b, pt: (0,) * a.ndim)
    return pl.pallas_call(
        functools.partial(_compress_pages_kernel, n_pages=n_pages), name="compress_pages",
        grid_spec=pltpu.PrefetchScalarGridSpec(
            num_scalar_prefetch=1, grid=(DB,),
            in_specs=[pl.BlockSpec(memory_space=pl.ANY), full(pos2), full(w1p), full(w2p)],
            out_specs=pl.BlockSpec((None, 2, 128, KVD), lambda b, pt: (b, 0, 0, 0)),
            scratch_shapes=[pltpu.VMEM((2, CMP_ROWS, KVD), F32), pltpu.VMEM((256 * BLOCK_PITCH, KVD), F32),
                            pltpu.VMEM((256, KVD), F32), pltpu.SemaphoreType.DMA((2,))]),
        out_shape=jax.ShapeDtypeStruct((DB, 2, 128, KVD), BF16),
        compiler_params=_cparams(("arbitrary",)),
    )(page_table.reshape(-1), pool_t, pos2, w1p, w2p)


def _topk_extract(v, k, want_idx, axis=1):
    pos = lax.broadcasted_iota(jnp.int32, v.shape, axis).astype(F32)
    idx = []
    for _ in range(k):
        mx = jnp.max(v, axis=axis, keepdims=True)
        first = jnp.min(jnp.where(v == mx, pos, 1e9), axis=axis, keepdims=True)
        v = jnp.where(pos == first, -jnp.inf, v)
        if want_idx:
            idx.append(first)
    return v, idx


def _cmp_kernel(q_ref, kc_ref, vc_ref, gates_ref, nb_ref, oc_ref, m_ref, *, nc):
    qt = pl.program_id(1)
    t0 = qt * TQ
    half = nc // 2
    gates = gates_ref[...]

    def body(w):
        rows = lambda ref: jnp.concatenate([ref[0:w, :], ref[half:half + w, :]], axis=0)
        lane = lax.broadcasted_iota(jnp.int32, (2 * w, LANES), 1)
        crow = lax.broadcasted_iota(jnp.int32, (2 * w, LANES), 0)
        cnat = jnp.where(crow < w, 2 * crow, 2 * (crow - w) + 1)
        cbase = t0 // L_CMP - 8
        e = jnp.where(lane < 16, lane, lane - 16)
        ext = jnp.where((lane < 32) & (cnat == cbase + e), 1.0, 0.0)
        ext = jnp.where((lane == 32) & (cnat >= cbase + 16), NEG, ext).astype(BF16)
        lo = lane < HEAD_DIM
        kc = rows(kc_ref)
        vc = rows(vc_ref)
        zero = jnp.zeros_like(kc)
        rhs = [jnp.concatenate([jnp.where(lo, kc, zero), ext], axis=1),
               jnp.concatenate([jnp.where(lo, zero, kc), ext], axis=1)]
        vg = [jnp.where(lo, vc, zero), jnp.where(lo, zero, vc)]
        imp = [jnp.zeros((TQ, 2 * w), F32), jnp.zeros((TQ, 2 * w), F32)]

        def scores(hh):
            qs = q_ref[:, (hh // 2) * LANES:(hh // 2 + 1) * LANES]
            return _dot_nt(jnp.concatenate([qs, nb_ref[hh]], axis=1), rhs[hh % 2])

        s_next = scores(0)
        osl = None
        for hh in range(N_HEADS):
            g = hh % 2
            s = s_next
            if hh + 1 < N_HEADS:
                s_next = scores(hh + 1)
            m = jnp.max(s, axis=-1, keepdims=True)
            ex = jnp.exp2(s - m)
            l = jnp.sum(ex, axis=-1, keepdims=True)
            p = ex * jnp.where(m > 0.1 * NEG, 1.0 / l, 0.0)
            imp[g] = imp[g] + p
            o = gates[:, hh:hh + 1] * _dot(p.astype(BF16), vg[g])
            if g == 0:
                osl = o
            else:
                oc_ref[:, (hh // 2) * LANES:(hh // 2 + 1) * LANES] = (osl + o).astype(BF16)
        blk = lax.broadcasted_iota(jnp.int32, (w, TQ), 0)
        tcol = t0 + lax.broadcasted_iota(jnp.int32, (w, TQ), 1)
        forced = (blk == tcol // L_SEL) | (blk == 0)
        started = blk * L_SEL <= tcol
        for g in range(N_KV):
            v = (imp[g][:, :w] + imp[g][:, w:]).T
            v = jnp.where(forced, FORCED_SCORE, jnp.where(started, v, -1.0))
            v, _ = _topk_extract(v, N_SEL, False, axis=0)
            m_ref[:, g * half:g * half + w] = jnp.where(v == -jnp.inf, 0.0, NEG).T.astype(BF16)
            if w < half:
                m_ref[:, g * half + w:(g + 1) * half] = jnp.full((TQ, half - w), NEG, BF16)

    w_short = half // 2
    if w_short % LANES == 0 and w_short >= N_SEL:
        n_short = w_short * L_SEL // TQ

        @pl.when(qt < n_short)
        def _():
            body(w_short)

        @pl.when(qt >= n_short)
        def _():
            body(half)
    else:
        body(half)


def _cmp_call(q, kc, vc, gates, nbt):
    B, S, _ = q.shape
    nc = kc.shape[1]
    row = lambda w: pl.BlockSpec((None, TQ, w), lambda b, t: (b, t, 0))
    perb = lambda a: pl.BlockSpec((None,) + a.shape[1:], lambda b, t: (b,) + (0,) * (a.ndim - 1))
    full = lambda a: pl.BlockSpec(a.shape, lambda b, t: (0,) * a.ndim)
    return pl.pallas_call(
        functools.partial(_cmp_kernel, nc=nc), name="cmp", grid=(B, S // TQ),
        in_specs=[row(QD), perb(kc), perb(vc), row(LANES), full(nbt)],
        out_specs=[row(QD), row(nc)],
        out_shape=[jax.ShapeDtypeStruct((B, S, QD), BF16), jax.ShapeDtypeStruct((B, S, nc), BF16)],
        compiler_params=_cparams(("arbitrary", "arbitrary")),
    )(q, kc, vc, gates, nbt)


def _sel_kernel(q_ref, m_ref, gates_ref, k_ref, v_ref, oh_ref, bt_ref, os_ref, lhs_s, acc_s, m_s, *, nh):
    qt = pl.program_id(1)
    tph = HALF_KEYS // KT
    for j in range(GROUP):
        qs = q_ref[:, j * LANES:(j + 1) * LANES]
        for h in range(nh):
            for g in range(N_KV):
                c0 = (g * nh + h) * LANES
                lhs_s[(h * 2 + g) * GROUP + j] = jnp.concatenate([qs, m_ref[:, c0:c0 + LANES]], axis=1)
    lo_q = lax.broadcasted_iota(jnp.int32, (TQ, LANES), 1) < HEAD_DIM

    def step(kt, nk, kind):
        rows = nk * KT
        k_t = k_ref[pl.ds(pl.multiple_of(kt * KT, KT), rows), :]
        v_t = v_ref[pl.ds(pl.multiple_of(kt * KT, KT), rows), :]
        h = kt // tph
        oh = oh_ref[pl.ds(pl.multiple_of((kt - h * tph) * KT, KT), rows), :]
        lane = lax.broadcasted_iota(jnp.int32, (rows, LANES), 1)
        lo_k = lane < HEAD_DIM
        zero = jnp.zeros_like(k_t)
        rhs = [jnp.concatenate([jnp.where(lo_k, k_t, zero), oh], axis=1),
               jnp.concatenate([jnp.where(lo_k, zero, k_t), oh], axis=1)]
        vg = [jnp.where(lo_k, v_t, jnp.where(lane == HEAD_DIM, 1.0, 0.0).astype(BF16)),
              jnp.where(lo_k, jnp.where(lane == 0, 1.0, 0.0).astype(BF16), v_t)]
        def scores(hh):
            g = hh % 2
            return _dot_nt(lhs_s[(h * 2 + g) * GROUP + hh // 2], rhs[g])

        s_next = scores(0)
        for hh in range(N_HEADS):
            s = s_next
            if hh + 1 < N_HEADS:
                s_next = scores(hh + 1)
            if kind:
                near, diag = bt_ref[0, hh], bt_ref[1, hh]
                s00, s01, s10, s11 = s[:SUB, :SUB], s[:SUB, SUB:], s[SUB:, :SUB], s[SUB:, SUB:]
                if kind == 1:
                    top, bot = [s00, s01 + near], [s10, s11]
                else:
                    top, bot = [s00 + diag, jnp.full((SUB, SUB), NEG, F32)], [s10 + near, s11 + diag]
                s = jnp.concatenate([jnp.concatenate(top, axis=1), jnp.concatenate(bot, axis=1)], axis=0)
            mc = jnp.broadcast_to(jnp.max(s, axis=-1, keepdims=True), (TQ, LANES))
            if kind == 2:
                mn = mc
            else:
                mp = m_s[hh]
                mn = jnp.maximum(mp, mc)
            m_s[hh] = mn
            p = jnp.exp2(s - jnp.concatenate([mn] * (rows // LANES), axis=1))
            pv = _dot(p.astype(BF16), vg[hh % 2])
            acc_s[hh] = pv if kind == 2 else acc_s[hh] * jnp.exp2(mp - mn) + pv

    step(qt, 1, 2)

    @pl.when(qt >= 1)
    def _():
        step(qt - 1, 1, 1)

    nfar = jnp.maximum(qt - 1, 0)
    nbig = nfar // FAR_TILES

    def big(i, c):
        step(i * FAR_TILES, FAR_TILES, False)
        return c

    def small(kt, c):
        step(kt, 1, False)
        return c

    lax.fori_loop(0, nbig, big, 0)
    lax.fori_loop(nbig * FAR_TILES, nfar, small, 0)

    gates = gates_ref[...]
    for j in range(GROUP):
        a0 = acc_s[2 * j]
        a1 = acc_s[2 * j + 1]
        c = N_HEADS + 2 * j
        sc0 = gates[:, c:c + 1] / a0[:, HEAD_DIM:HEAD_DIM + 1]
        sc1 = gates[:, c + 1:c + 2] / a1[:, 0:1]
        os_ref[:, j * LANES:(j + 1) * LANES] = jnp.where(lo_q, a0 * sc0, a1 * sc1).astype(BF16)


def _sel_call(q, msk, gates, kb, vb, onehot, btab):
    B, S, _ = q.shape
    nh = msk.shape[2] // (2 * LANES)
    row = lambda w: pl.BlockSpec((None, TQ, w), lambda b, t: (b, t, 0))
    perb = lambda a: pl.BlockSpec((None,) + a.shape[1:], lambda b, t: (b,) + (0,) * (a.ndim - 1))
    full = lambda a: pl.BlockSpec(a.shape, lambda b, t: (0,) * a.ndim)
    return pl.pallas_call(
        functools.partial(_sel_kernel, nh=nh), name="sel", grid=(B, S // TQ),
        in_specs=[row(QD), row(msk.shape[2]), row(LANES), perb(kb), perb(vb), full(onehot), full(btab)],
        out_specs=row(QD),
        out_shape=jax.ShapeDtypeStruct((B, S, QD), BF16),
        scratch_shapes=[pltpu.VMEM((nh * 2 * GROUP, TQ, 2 * LANES), BF16), pltpu.VMEM((N_HEADS, TQ, LANES), F32),
                        pltpu.VMEM((N_HEADS, TQ, LANES), F32)],
        compiler_params=_cparams(("arbitrary", "arbitrary")),
    )(q, msk, gates, kb, vb, onehot, btab)


def _win_kernel(q_ref, gates_ref, k0_ref, k1_ref, k2_ref, v0_ref, v1_ref, v2_ref, wt_ref, ow_ref):
    qt = pl.program_id(1)
    lo_k = lax.broadcasted_iota(jnp.int32, (3 * KT, LANES), 1) < HEAD_DIM
    kcat = jnp.concatenate([k0_ref[...], k1_ref[...], k2_ref[...]], axis=0)
    vcat = jnp.concatenate([v0_ref[...], v1_ref[...], v2_ref[...]], axis=0)
    lane_k = lax.broadcasted_iota(jnp.int32, (3 * KT, LANES), 1)
    zero = jnp.zeros_like(kcat)
    kg = [jnp.where(lo_k, kcat, zero), jnp.where(lo_k, zero, kcat)]
    vg = [jnp.where(lo_k, vcat, jnp.where(lane_k == HEAD_DIM, 1.0, 0.0).astype(BF16)),
          jnp.where(lo_k, jnp.where(lane_k == 0, 1.0, 0.0).astype(BF16), vcat)]
    lo_q = lax.broadcasted_iota(jnp.int32, (TQ, LANES), 1) < HEAD_DIM
    gates = gates_ref[...]
    ii = lax.broadcasted_iota(jnp.int32, (SUB, SUB), 0)
    jj = lax.broadcasted_iota(jnp.int32, (SUB, SUB), 1)
    far_edge = jnp.where(jj >= ii, 0.0, NEG)
    outside = jnp.full((SUB, SUB), NEG, F32)

    def scores(hh):
        return _dot_nt(q_ref[:, (hh // 2) * LANES:(hh // 2 + 1) * LANES], kg[hh % 2])

    def body(edge):
        if edge:
            lane = lax.broadcasted_iota(jnp.int32, (1, 3 * KT), 1)
            pen = jnp.where(lane < KT, jnp.where(qt >= 2, 0.0, NEG),
                            jnp.where(lane < 2 * KT, jnp.where(qt >= 1, 0.0, NEG), 0.0))
        s_next = scores(0)
        pv = [None, None]
        for hh in range(N_HEADS):
            j, g = hh // 2, hh % 2
            s = s_next
            if hh + 1 < N_HEADS:
                s_next = scores(hh + 1)
            near, diag = wt_ref[0, hh], wt_ref[1, hh]
            col = lambda a, c: s[a * SUB:(a + 1) * SUB, c * SUB:(c + 1) * SUB]
            top = [col(0, 0) + far_edge, col(0, 1), col(0, 2), col(0, 3) + near, col(0, 4) + diag, outside]
            bot = [outside, col(1, 1) + far_edge, col(1, 2), col(1, 3), col(1, 4) + near, col(1, 5) + diag]
            s = jnp.concatenate([jnp.concatenate(top, axis=1), jnp.concatenate(bot, axis=1)], axis=0)
            if edge:
                s = s + pen
            ex = jnp.exp2(s - jnp.max(s, axis=-1, keepdims=True))
            pv[g] = _dot(ex.astype(BF16), vg[g])
            if g == 1:
                c = 2 * N_HEADS + 2 * j
                sc0 = gates[:, c:c + 1] / pv[0][:, HEAD_DIM:HEAD_DIM + 1]
                sc1 = gates[:, c + 1:c + 2] / pv[1][:, 0:1]
                ow_ref[:, j * LANES:(j + 1) * LANES] = jnp.where(lo_q, pv[0] * sc0, pv[1] * sc1).astype(BF16)

    @pl.when(qt >= 2)
    def _():
        body(False)

    @pl.when(qt < 2)
    def _():
        body(True)


def _win_call(q, gates, kb, vb, wtab):
    B, S, _ = q.shape
    row = lambda w: pl.BlockSpec((None, TQ, w), lambda b, t: (b, t, 0))
    back = lambda d: pl.BlockSpec((None, KT, KVD), lambda b, t: (b, jnp.maximum(t - d, 0), 0))
    full = lambda a: pl.BlockSpec(a.shape, lambda b, t: (0,) * a.ndim)
    return pl.pallas_call(
        _win_kernel, name="win", grid=(B, S // TQ),
        in_specs=[row(QD), row(LANES), back(2), back(1), back(0), back(2), back(1), back(0), full(wtab)],
        out_specs=row(QD),
        out_shape=jax.ShapeDtypeStruct((B, S, QD), BF16),
        compiler_params=_cparams(("arbitrary", "arbitrary")),
    )(q, gates, kb, kb, kb, vb, vb, vb, wtab)


def _merge_kernel(za_ref, oc_ref, os_ref, ow_ref, ga_ref, gb_ref, x_ref, wuc_ref, wua_ref, wo_ref, g_ref, b_ref,
                  h_ref):
    a = _dot(za_ref[...], wuc_ref[...])
    ob = oc_ref[...].astype(F32) + os_ref[...].astype(F32) + ow_ref[...].astype(F32)
    bm = _dot(ob.astype(BF16), wua_ref[...])
    pre = ga_ref[...].astype(F32) * a + gb_ref[...].astype(F32) * bm
    mix = _dot(pre.astype(BF16), wo_ref[...])
    h_ref[...] = _layer_norm(ALPHA * x_ref[...] + mix, g_ref[...], b_ref[...])


def _merge_call(za, oc, os_, ow, ga, gb, x, wuc, wua, wo, g, b, tm):
    R = x.shape[0]
    row = lambda w: pl.BlockSpec((tm, w), lambda i: (i, 0))
    full = lambda a: pl.BlockSpec(a.shape, lambda i: (0,) * a.ndim)
    return pl.pallas_call(
        _merge_kernel, name="merge", grid=(R // tm,),
        in_specs=[row(D_CONV), row(QD), row(QD), row(QD), row(D_MODEL), row(D_MODEL), row(D_MODEL),
                  full(wuc), full(wua), full(wo), full(g), full(b)],
        out_specs=row(D_MODEL), out_shape=jax.ShapeDtypeStruct((R, D_MODEL), F32),
        compiler_params=_cparams(("arbitrary",)),
    )(za, oc, os_, ow, ga, gb, x, wuc, wua, wo, g, b)


def _ffn_kernel(h_ref, wg_ref, wu_ref, wd_ref, g_ref, b_ref, y_ref):
    h = h_ref[...]
    hb = h.astype(BF16)
    act = jax.nn.silu(_dot(hb, wg_ref[...])) * _dot(hb, wu_ref[...])
    f = _dot(act.astype(BF16), wd_ref[...])
    y_ref[...] = _layer_norm(ALPHA * h + f, g_ref[...], b_ref[...])


def _ffn_call(h, wg, wu, wd, g, b, tm):
    R = h.shape[0]
    row = lambda w: pl.BlockSpec((tm, w), lambda i: (i, 0))
    full = lambda a: pl.BlockSpec(a.shape, lambda i: (0,) * a.ndim)
    return pl.pallas_call(
        _ffn_kernel, name="ffn", grid=(R // tm,),
        in_specs=[row(D_MODEL), full(wg), full(wu), full(wd), full(g), full(b)],
        out_specs=row(D_MODEL), out_shape=jax.ShapeDtypeStruct((R, D_MODEL), F32),
        compiler_params=_cparams(("arbitrary",)),
    )(h, wg, wu, wd, g, b)


def _scmp_kernel(qh_ref, kc_ref, vc_ref, gate_ref, bc_ref, oc_ref, imp_ref, *, nc):
    half = nc // 2
    s = _dot_nt(qh_ref[...], kc_ref[...]) + bc_ref[...]
    m = jnp.max(s, axis=-1, keepdims=True)
    ex = jnp.exp2(s - m)
    p = ex * (1.0 / jnp.sum(ex, axis=-1, keepdims=True))
    o = _dot(p.astype(BF16), vc_ref[...]) * gate_ref[...]
    oc_ref[...] = o[0:GROUP] + o[GROUP:2 * GROUP]
    for g in range(N_KV):
        pg = jnp.sum(p[g * GROUP:(g + 1) * GROUP], axis=0, keepdims=True)
        imp_ref[g:g + 1, :] = pg[:, :half] + pg[:, half:]


def _scmp_call(qh, kc, vc, gate_c, bc):
    DB, _, nc, _ = (kc.shape[0], None, kc.shape[1], None)
    perb = lambda a: pl.BlockSpec((None,) + a.shape[1:], lambda b: (b,) + (0,) * (a.ndim - 1))
    full = lambda a: pl.BlockSpec(a.shape, lambda b: (0,) * a.ndim)
    return pl.pallas_call(
        functools.partial(_scmp_kernel, nc=nc), name="sample_cmp", grid=(DB,),
        in_specs=[perb(qh), perb(kc), perb(vc), perb(gate_c), full(bc)],
        out_specs=[pl.BlockSpec((None, GROUP, LANES), lambda b: (b, 0, 0)),
                   pl.BlockSpec((None, N_KV, nc // 2), lambda b: (b, 0, 0))],
        out_shape=[jax.ShapeDtypeStruct((DB, GROUP, LANES), F32), jax.ShapeDtypeStruct((DB, N_KV, nc // 2), F32)],
        compiler_params=_cparams(("arbitrary",)),
    )(qh, kc, vc, gate_c, bc)


def _stopk_kernel(imp_ref, idx_ref, *, k):
    v = imp_ref[...]
    lane = lax.broadcasted_iota(jnp.int32, v.shape, 1)
    v = jnp.where(lane == 0, FORCED_SCORE, v)
    _, idx = _topk_extract(v, k, True)
    out = jnp.zeros(v.shape, F32)
    for i, col in enumerate(idx):
        out = jnp.where(lane == i, col, out)
    idx_ref[...] = out.astype(jnp.int32)


def _stopk_call(imp, k):
    return pl.pallas_call(
        functools.partial(_stopk_kernel, k=k), name="sample_topk",
        out_shape=jax.ShapeDtypeStruct(imp.shape, jnp.int32),
        compiler_params=pltpu.CompilerParams(vmem_limit_bytes=VMEM_LIMIT),
    )(imp)


def _ssel_kernel(idx_ref, pt_ref, q_ref, knew_ref, vnew_ref, kwnew_ref, vwnew_ref, kwcol_ref, vwcol_ref,
                 gate_s_ref, gate_w_ref, kwin_ref, vwin_ref, tn_ref, tw_ref, b0_ref, kpool_ref, vpool_ref,
                 os_ref, ow_ref, kwo_ref, vwo_ref, kbuf, vbuf, sem, *, n_pages, nsel, n_last):
    b = pl.program_id(0)
    nb = pl.num_programs(0)
    slot = lax.rem(b, 2)

    def copies(bb, sl, g, s):
        n = idx_ref[(bb * N_KV + g) * nsel + s]
        page = pt_ref[bb * n_pages + n // 2]
        dst = pl.ds(s * PAGE_SIZE, PAGE_SIZE)
        return (pltpu.make_async_copy(kpool_ref.at[page, g], kbuf.at[sl, g, :, dst], sem.at[sl, 0]),
                pltpu.make_async_copy(vpool_ref.at[page, g], vbuf.at[sl, g, :, dst], sem.at[sl, 1]))

    def start_all(bb, sl):
        for g in range(N_KV):
            for s in range(nsel):
                ck, cv = copies(bb, sl, g, s)
                ck.start()
                cv.start()

    @pl.when(b == 0)
    def _():
        start_all(b, 0)

    @pl.when(b + 1 < nb)
    def _():
        start_all(b + 1, 1 - slot)

    def attend(qg, s, k_new, v_new, vt, bias_new, gate):
        s_new = jnp.sum(qg.astype(F32) * k_new.astype(BF16).astype(F32), axis=-1, keepdims=True) + bias_new
        m = jnp.maximum(jnp.max(s, axis=-1, keepdims=True), s_new)
        ex = jnp.exp2(s - m)
        ex_new = jnp.exp2(s_new - m)
        l = jnp.sum(ex, axis=-1, keepdims=True) + ex_new
        o = _dot_nt(ex.astype(BF16), vt.astype(BF16)) + ex_new.astype(BF16).astype(F32) * v_new.astype(BF16).astype(F32)
        return o * (gate / l)

    wb = kwin_ref.shape[-1]
    lane_w = lax.broadcasted_iota(jnp.int32, (HEAD_DIM, wb), 1)
    for g in range(N_KV):
        qg = q_ref[g]
        kw = kwin_ref[g]
        vw = vwin_ref[g]
        s = _dot(qg, kw.astype(BF16)) + tw_ref[g]
        ow_ref[g] = attend(qg, s, kwnew_ref[g], vwnew_ref[g], vw, b0_ref[g], gate_w_ref[g])
        kwo_ref[g] = jnp.where(lane_w == wb - 1, kwcol_ref[g], pltpu.roll(kw, wb - 1, 1))
        vwo_ref[g] = jnp.where(lane_w == wb - 1, vwcol_ref[g], pltpu.roll(vw, wb - 1, 1))

    for g in range(N_KV):
        for s in range(nsel):
            ck, cv = copies(b, slot, g, s)
            ck.wait()
            cv.wait()

    upper = lax.broadcasted_iota(jnp.int32, (GROUP, PAGE_SIZE), 1) >= L_SEL
    keep_upper = jnp.where(upper, 0.0, NEG)
    keep_lower = jnp.where(upper, NEG, 0.0)
    for g in range(N_KV):
        qg = q_ref[g]
        pieces = []
        for s in range(nsel):
            n = idx_ref[(b * N_KV + g) * nsel + s]
            piece = jnp.where(n % 2 == 1, keep_upper, keep_lower)
            piece = piece + jnp.where(n == n_last, 1.0, 0.0) * tn_ref[0, g]
            piece = piece + jnp.where(n == n_last - 1, 1.0, 0.0) * tn_ref[1, g]
            pieces.append(piece)
        s = _dot(qg, kbuf[slot, g].astype(BF16)) + jnp.concatenate(pieces, axis=1)
        os_ref[g] = attend(qg, s, knew_ref[g], vnew_ref[g], vbuf[slot, g], b0_ref[g], gate_s_ref[g])


def _ssel_call(idx, page_table, qg, knew, vnew, kwnew, vwnew, kwcol, vwcol, gate_s, gate_w, kwin, vwin, tn, tw, b0,
               kpool, vpool):
    DB, n_pages = page_table.shape
    nsel = idx.shape[-1]
    wb = kwin.shape[-1]
    n_last = n_pages * (PAGE_SIZE // L_SEL) - 1
    perb = lambda a: pl.BlockSpec((None,) + a.shape[1:], lambda b, *_: (b,) + (0,) * (a.ndim - 1))
    full = lambda a: pl.BlockSpec(a.shape, lambda b, *_: (0,) * a.ndim)
    anyspec = pl.BlockSpec(memory_space=pl.ANY)
    head_o = pl.BlockSpec((None, N_KV, GROUP, HEAD_DIM), lambda b, *_: (b, 0, 0, 0))
    win_o = pl.BlockSpec((None, N_KV, HEAD_DIM, wb), lambda b, *_: (b, 0, 0, 0))
    head_s = jax.ShapeDtypeStruct((DB, N_KV, GROUP, HEAD_DIM), F32)
    win_s = jax.ShapeDtypeStruct((DB, N_KV, HEAD_DIM, wb), F32)
    return pl.pallas_call(
        functools.partial(_ssel_kernel, n_pages=n_pages, nsel=nsel, n_last=n_last), name="sample_sel_win",
        grid_spec=pltpu.PrefetchScalarGridSpec(
            num_scalar_prefetch=2, grid=(DB,),
            in_specs=[perb(qg), perb(knew), perb(vnew), perb(kwnew), perb(vwnew), perb(kwcol), perb(vwcol),
                      perb(gate_s), perb(gate_w), perb(kwin), perb(vwin), full(tn), full(tw), full(b0),
                      anyspec, anyspec],
            out_specs=[head_o, head_o, win_o, win_o],
            scratch_shapes=[pltpu.VMEM((2, N_KV, HEAD_DIM, nsel * PAGE_SIZE), F32),
                            pltpu.VMEM((2, N_KV, HEAD_DIM, nsel * PAGE_SIZE), F32),
                            pltpu.SemaphoreType.DMA((2, 2))]),
        out_shape=[head_s, head_s, win_s, win_s],
        compiler_params=_cparams(("arbitrary",)),
    )(idx.reshape(-1), page_table.reshape(-1), qg, knew, vnew, kwnew, vwnew, kwcol, vwcol, gate_s, gate_w,
      kwin, vwin, tn, tw, b0, kpool, vpool)


def _perm_w_in(w_in):
    o_q = 3 * D_CONV
    o_kv = o_q + QD
    o_ng = o_kv + 6 * KVD
    o_mg = o_ng + 3 * N_HEADS
    q = w_in[:, o_q:o_kv].reshape(D_MODEL, N_KV, GROUP, HEAD_DIM).transpose(0, 2, 1, 3).reshape(D_MODEL, QD)
    q = q * (HEAD_DIM ** -0.5 * LOG2E)
    ng = w_in[:, o_ng:o_mg].reshape(D_MODEL, N_KV, GROUP, 3).transpose(0, 3, 2, 1).reshape(D_MODEL, 3 * N_HEADS)
    ng = jnp.pad(ng, ((0, 0), (0, LANES - 3 * N_HEADS)))
    return jnp.concatenate([w_in[:, :o_q], q, w_in[:, o_kv:o_ng], ng, w_in[:, o_mg:]], axis=1).astype(BF16)


def _perm_heads(t):
    return t.reshape(t.shape[:-1] + (N_KV, GROUP)).swapaxes(-1, -2).reshape(t.shape)


def _cmp_weights(w1, w2, pos):
    eye = jnp.eye(N_KV, dtype=F32)
    w1p = jnp.einsum('ldh,ab->ladbh', w1.reshape(L_CMP, HEAD_DIM, CMP_HID), eye)
    w1p = w1p.reshape(L_CMP // 2, 2 * KVD, N_KV * CMP_HID).astype(BF16)
    w2p = jnp.einsum('hd,ab->ahbd', w2, eye).reshape(N_KV * CMP_HID, KVD).astype(BF16)
    pos2 = jnp.tile(pos[:, None, :], (1, N_KV, 1)).reshape(L_CMP, 1, KVD)
    return pos2, w1p, w2p


def kernel(x_prompt, x_sample, cache_k_cmp, cache_v_cmp, cache_k_slc, cache_v_slc, state_k_win, state_v_win,
           state_conv, page_table, w_in, conv_w, w_cmp_k1, w_cmp_k2, pos_cmp_k, w_cmp_v1, w_cmp_v2, pos_cmp_v,
           rel_bias, w_up_conv, w_up_attn, w_o, ln1_g, ln1_b, w_gate_ffn, w_up_ffn, w_down_ffn, ln2_g, ln2_b):
    B, S, _ = x_prompt.shape
    DB = x_sample.shape[0]
    n_pages = page_table.shape[1]
    P = n_pages * PAGE_SIZE
    n_pool = cache_k_cmp.shape[1]
    wb = state_k_win.shape[2]
    assert x_sample.shape[1] == 1 and w_in.shape[0] == 1
    assert S % HALF_KEYS == 0 and P == HALF_KEYS and wb == WINDOW and S >= WINDOW
    nc = S // L_CMP
    nh = S // HALF_KEYS

    w_perm = _perm_w_in(w_in[0])
    cw = conv_w[0]
    cmp_k = _cmp_weights(w_cmp_k1[0], w_cmp_k2[0], pos_cmp_k[0])
    cmp_v = _cmp_weights(w_cmp_v1[0], w_cmp_v2[0], pos_cmp_v[0])
    wuc = w_up_conv[0].astype(BF16)
    wua = w_up_attn[0].reshape(N_KV, GROUP, HEAD_DIM, D_MODEL).transpose(1, 0, 2, 3).reshape(QD, D_MODEL).astype(BF16)
    wo = w_o[0].astype(BF16)
    wg, wu, wd = w_gate_ffn[0].astype(BF16), w_up_ffn[0].astype(BF16), w_down_ffn[0].astype(BF16)
    g1, b1, g2, b2 = ln1_g, ln1_b, ln2_g, ln2_b

    bucket = _bucket_table()
    bias_p = (rel_bias[bucket] - rel_bias[N_BUCKETS - 1][None, :]) * LOG2E
    bias_pp = _perm_heads(bias_p)
    ii = np.arange(TQ)[:, None]
    dist_c = ii - L_CMP * np.arange(16)[None, :] + 8 * L_CMP - (L_CMP - 1)
    nb = _bias_from_dist(bias_pp, dist_c, True)
    nb_hi = nb.astype(BF16)
    nb_lo = jnp.where(jnp.asarray(dist_c < 0), 0.0, nb - nb_hi.astype(F32)).astype(BF16)
    nbt = jnp.concatenate([nb_hi, nb_lo, jnp.ones((N_HEADS, TQ, 1), BF16),
                           jnp.zeros((N_HEADS, TQ, LANES - 33), BF16)], axis=-1)
    wtab =jnp.stack([_toeplitz(bias_pp, SUB, False, SUB), _toeplitz(bias_pp, 0, True, SUB)])
    onehot = jnp.asarray((np.arange(HALF_KEYS)[:, None] // L_SEL == np.arange(LANES)[None, :]), BF16)

    (za, q, kcm, vcm, ksl, vsl, kwn, vwn, kslb, vslb, kwnb, vwnb, gates, ga, gb, conv_p) = _proj_call(
        x_prompt, w_perm, cw, None, 256)
    kc = _compress_prompt(kcm, *cmp_k).reshape(B, nc, KVD)
    vc = _compress_prompt(vcm, *cmp_v).reshape(B, nc, KVD)
    oc, msk = _cmp_call(q, kc, vc, gates, nbt)
    os_ = _sel_call(q, msk, gates, kslb, vslb, onehot, wtab)
    ow = _win_call(q, gates, kwnb, vwnb, wtab)
    f2 = lambda t: t.reshape(B * S, t.shape[-1])
    h = _merge_call(f2(za), f2(oc), f2(os_), f2(ow), f2(ga), f2(gb), f2(x_prompt), wuc, wua, wo, g1, b1, 256)
    y_p = _ffn_call(h, wg, wu, wd, g2, b2, 256).reshape(B, S, D_MODEL)
    kv5 = lambda t: t.reshape(1, B, S, N_KV, HEAD_DIM)
    keep = min(WINDOW, S)
    prompt_state = (kv5(kcm), kv5(vcm), kv5(ksl), kv5(vsl), kv5(kwn)[:, :, S - keep:], kv5(vwn)[:, :, S - keep:],
                    conv_p[None])

    xs = x_sample.reshape(1, DB, D_MODEL)
    p0 = state_conv[0, :, 0, :][None]
    p1 = state_conv[0, :, 1, :][None]
    (za_s, q_s, kcm_s, vcm_s, ksl_s, vsl_s, kwn_s, vwn_s, gates_s, ga_s, gb_s, u_s) = _proj_call(
        xs, w_perm, cw, (p0, p1), DB)
    pool_rows = lambda c: jnp.transpose(c[0], (0, 2, 3, 1)).reshape(n_pool, KVD, PAGE_SIZE)
    kc_s = _compress_pages(page_table, pool_rows(cache_k_cmp), *cmp_k).reshape(DB, P // L_CMP, KVD)
    vc_s = _compress_pages(page_table, pool_rows(cache_v_cmp), *cmp_v).reshape(DB, P // L_CMP, KVD)
    lane_lo = jnp.arange(LANES) < HEAD_DIM
    q3 = q_s.reshape(DB, GROUP, LANES)
    qh = jnp.concatenate([jnp.where(lane_lo, q3, 0), jnp.where(lane_lo, 0, q3)], axis=1)
    gsm = gates_s.reshape(DB, LANES)[:, :3 * N_HEADS].reshape(DB, 3, GROUP, N_KV)
    gsm = gsm.transpose(1, 0, 3, 2).reshape(3, DB, N_HEADS, 1)
    row_lo = (jnp.arange(N_HEADS) < GROUP)[:, None]
    lane_own = jnp.where(row_lo, lane_lo[None, :], ~lane_lo[None, :])
    gate3 = jnp.where(lane_own, gsm, 0.0)
    ncs = P // L_CMP
    cnat = np.concatenate([np.arange(0, ncs, 2), np.arange(1, ncs, 2)])
    bc = _bias_from_dist(bias_p, P - (L_CMP * cnat + L_CMP - 1), False)
    oc_s, imp_s = _scmp_call(qh, kc_s, vc_s, gate3[0], bc)
    nsel = N_SEL - 1
    idx = _stopk_call(imp_s.reshape(DB * N_KV, ncs // 2), nsel)[:, :nsel]
    hg = lambda t: t.reshape((N_KV, GROUP) + t.shape[1:])
    zpad = jnp.zeros((N_HEADS, L_SEL), F32)
    t_last = _bias_from_dist(bias_p, L_SEL - np.arange(L_SEL), False)
    t_prev = _bias_from_dist(bias_p, 2 * L_SEL - np.arange(L_SEL), False)
    tn = jnp.stack([hg(jnp.concatenate([zpad, t_last], 1)), hg(jnp.concatenate([t_prev, zpad], 1))])
    tw = hg(_bias_from_dist(bias_p, wb - np.arange(wb), False))
    b0 = hg(bias_p[0][:, None])
    qg = q_s.reshape(DB, GROUP, N_KV, HEAD_DIM).transpose(0, 2, 1, 3)
    gate_g = gsm.reshape(3, DB, N_KV, GROUP, 1)
    rowv = lambda t: t.reshape(DB, N_KV, 1, HEAD_DIM)
    colv = lambda t: t.reshape(DB, N_KV, HEAD_DIM, 1)
    tpose = lambda t: jnp.transpose(t[0], (0, 2, 3, 1))
    os_s, ow_s, kw_out, vw_out = _ssel_call(
        idx, page_table, qg, rowv(ksl_s), rowv(vsl_s), rowv(kwn_s), rowv(vwn_s), colv(kwn_s), colv(vwn_s),
        gate_g[1], gate_g[2], tpose(state_k_win), tpose(state_v_win), tn, tw, b0,
        tpose(cache_k_slc), tpose(cache_v_slc))
    fl = lambda t: t.reshape(DB, -1)
    slots = lambda t: t.transpose(0, 2, 1, 3).reshape(DB, QD)
    h_s = _merge_call(fl(za_s), fl(oc_s), slots(os_s), slots(ow_s), fl(ga_s), fl(gb_s), fl(xs), wuc, wua, wo,
                      g1, b1, DB)
    y_s = _ffn_call(h_s, wg, wu, wd, g2, b2, DB).reshape(DB, 1, D_MODEL)
    kv5s = lambda t: t.reshape(1, DB, 1, N_KV, HEAD_DIM)
    conv_s = jnp.stack([state_conv[0, :, 1, :], u_s.reshape(DB, D_CONV)], axis=1)[None]
    untp = lambda t: jnp.transpose(t, (0, 3, 1, 2))[None]
    sample_state = (kv5s(kcm_s), kv5s(vcm_s), kv5s(ksl_s), kv5s(vsl_s), untp(kw_out), untp(vw_out), conv_s)
    return (y_p, y_s) + prompt_state + sample_state
```

```python
import functools
import math

import numpy as np
import jax
import jax.numpy as jnp
from jax import lax
from jax.experimental import pallas as pl
from jax.experimental.pallas import tpu as pltpu

F32 = jnp.float32
BF16 = jnp.bfloat16

D_MODEL = 1024
D_CONV = 512
CONV_W = 3
N_HEADS = 16
HEAD_DIM = 64
N_KV = 2
GROUP = N_HEADS // N_KV
QD = N_HEADS * HEAD_DIM
KVD = N_KV * HEAD_DIM
L_CMP = 32
L_SEL = 64
N_SEL = 16
WINDOW = 512
CMP_HID = 256
FORCED_SCORE = 1e9
N_BUCKETS = 32
MAX_DIST = 128
D_FF = -(-8 * D_MODEL // (3 * 256)) * 256
DEPTH = 1
ALPHA = (2 * DEPTH) ** 0.25
LN_EPS = 1e-5
NEG = -1e30
PAGE_SIZE = 128
LOG2E = math.log2(math.e)

LANES = 128
VMEM_LIMIT = 56 * 1024 * 1024

C_CB, C_CC, C_CH, C_Q, C_KV, C_NG, C_MG, C_END = 0, 512, 1024, 1536, 2560, 3328, 3456, 5504

TQ = 256
KT = 256
HALF_KEYS = 8192
FAR_TILES = 4
SUB = 128


def _dot(a, b):
    return jnp.dot(a, b, preferred_element_type=F32)


def _dot_nt(a, b):
    return lax.dot_general(a, b, (((1,), (1,)), ((), ())), preferred_element_type=F32)


def _cparams(sem):
    return pltpu.CompilerParams(dimension_semantics=sem, vmem_limit_bytes=VMEM_LIMIT)


def _layer_norm(x, g, b):
    mu = jnp.mean(x, axis=-1, keepdims=True)
    xc = x - mu
    var = jnp.mean(xc * xc, axis=-1, keepdims=True)
    return xc * lax.rsqrt(var + LN_EPS) * g + b


def _bucket_table():
    d = np.arange(0, MAX_DIST + 1)
    exact = N_BUCKETS // 2
    log_ratio = (np.log(np.maximum(d, 1).astype(np.float32) / np.float32(exact))
                 / np.float32(math.log(MAX_DIST / exact))).astype(np.float32)
    large = np.minimum(exact + (log_ratio * np.float32(N_BUCKETS - exact)).astype(np.int32), N_BUCKETS - 1)
    return np.where(d < exact, d, large).astype(np.int32)


def _bias_from_dist(bias_p, dist, causal):
    dc = np.clip(dist, 0, MAX_DIST)
    val = jnp.moveaxis(bias_p[dc], -1, 0)
    if causal:
        val = jnp.where(jnp.asarray(dist < 0), NEG, val)
    return val


def _toeplitz(bias_p, off, causal, n=TQ):
    w = _bias_from_dist(bias_p, n - 1 - np.arange(2 * n - 1) + off, causal)
    w = jnp.pad(w, ((0, 0), (0, 1)))
    rows = jnp.tile(w, (1, n))[:, :n * (2 * n - 1)].reshape(w.shape[0], n, 2 * n - 1)
    return rows[:, :, n - 1:]


def _proj_kernel(*refs, tm, sample):
    if sample:
        (x_ref, w_ref, cw_ref, p0_ref, p1_ref, za_ref, q_ref, kcm_ref, vcm_ref, ksl_ref, vsl_ref, kwn_ref,
         vwn_ref, gates_ref, ga_ref, gb_ref, u_ref) = refs
    else:
        (x_ref, w_ref, cw_ref, za_ref, q_ref, kcm_ref, vcm_ref, ksl_ref, vsl_ref, kwn_ref, vwn_ref,
         kslb_ref, vslb_ref, kwnb_ref, vwnb_ref, gates_ref, ga_ref, gb_ref, conv_ref, ubuf) = refs
    xb = x_ref[...].astype(BF16)

    def mm(c0, c1):
        return _dot(xb, w_ref[:, c0:c1])

    u = mm(C_CB, C_CC) * mm(C_CH, C_Q)
    cc = mm(C_CC, C_CH)
    cw = cw_ref[...]
    if sample:
        u1 = p1_ref[...]
        u2 = p0_ref[...]
        u_ref[...] = u
    else:
        @pl.when(pl.program_id(1) == 0)
        def _():
            ubuf[0:8, :] = jnp.zeros((8, D_CONV), F32)
        ubuf[8:tm + 8, :] = u
        u1 = ubuf[7:tm + 7, :]
        u2 = ubuf[6:tm + 6, :]
        conv_ref[...] = ubuf[tm + 6:tm + 8, :]
        ubuf[0:8, :] = ubuf[tm:tm + 8, :]
    za_ref[...] = (cc * (cw[0:1] * u2 + cw[1:2] * u1 + cw[2:3] * u)).astype(BF16)
    q_ref[...] = mm(C_Q, C_KV).astype(BF16)
    kv = mm(C_KV, C_NG)
    for i, r in enumerate((kcm_ref, vcm_ref, ksl_ref, vsl_ref, kwn_ref, vwn_ref)):
        r[...] = kv[:, i * KVD:(i + 1) * KVD]
    if not sample:
        for i, r in enumerate((kslb_ref, vslb_ref, kwnb_ref, vwnb_ref)):
            r[...] = kv[:, (i + 2) * KVD:(i + 3) * KVD].astype(BF16)
    gates_ref[...] = jax.nn.sigmoid(mm(C_NG, C_MG))
    mg = jax.nn.sigmoid(mm(C_MG, C_END))
    ga_ref[...] = mg[:, :D_MODEL].astype(BF16)
    gb_ref[...] = mg[:, D_MODEL:].astype(BF16)


def _proj_call(x, w_perm, conv_w, prefix, tm):
    B, T, _ = x.shape
    sample = prefix is not None
    nt = T // tm
    row = lambda w: pl.BlockSpec((None, tm, w), lambda b, s: (b, s, 0))
    full = lambda a: pl.BlockSpec(a.shape, lambda b, s: (0,) * a.ndim)
    sds = lambda w, dt: jax.ShapeDtypeStruct((B, T, w), dt)
    in_specs = [row(D_MODEL), full(w_perm), full(conv_w)]
    args = [x, w_perm, conv_w]
    out_shape = [sds(D_CONV, BF16), sds(QD, BF16)] + [sds(KVD, F32)] * 6
    out_specs = [row(D_CONV), row(QD)] + [row(KVD)] * 6
    scratch = []
    if sample:
        in_specs += [row(D_CONV), row(D_CONV)]
        args += list(prefix)
    else:
        out_shape += [sds(KVD, BF16)] * 4
        out_specs += [row(KVD)] * 4
    out_shape += [sds(LANES, F32), sds(D_MODEL, BF16), sds(D_MODEL, BF16)]
    out_specs += [row(LANES), row(D_MODEL), row(D_MODEL)]
    if sample:
        out_shape.append(sds(D_CONV, F32))
        out_specs.append(row(D_CONV))
    else:
        out_shape.append(jax.ShapeDtypeStruct((B, CONV_W - 1, D_CONV), F32))
        out_specs.append(pl.BlockSpec((None, CONV_W - 1, D_CONV), lambda b, s: (b, 0, 0)))
        scratch.append(pltpu.VMEM((tm + 8, D_CONV), F32))
    return pl.pallas_call(
        functools.partial(_proj_kernel, tm=tm, sample=sample), name="proj",
        grid=(B, nt), in_specs=in_specs, out_specs=out_specs, out_shape=out_shape,
        scratch_shapes=scratch, compiler_params=_cparams(("arbitrary", "arbitrary")),
    )(*args)


CMP_ROWS = 256 * L_CMP


BLOCK_PITCH = 40


def _compress_rows(x_ref, pos_ref, w1_ref, w2_ref, tmp_ref, pitch):
    acc = None
    for lp in range(L_CMP // 2):
        xa = x_ref[pl.ds(2 * lp, 256, stride=pitch), :] + pos_ref[2 * lp]
        xb = x_ref[pl.ds(2 * lp + 1, 256, stride=pitch), :] + pos_ref[2 * lp + 1]
        t = _dot(jnp.concatenate([xa, xb], axis=1).astype(BF16), w1_ref[lp])
        acc = t if acc is None else acc + t
    hid = jax.nn.gelu(acc)
    tmp_ref[...] = _dot(hid.astype(BF16), w2_ref[...])
    return tmp_ref[pl.ds(0, 128, stride=2), :], tmp_ref[pl.ds(1, 128, stride=2), :]


def _compress_kernel(x_ref, pos_ref, w1_ref, w2_ref, o_ref, tmp_ref):
    ev, od = _compress_rows(x_ref, pos_ref, w1_ref, w2_ref, tmp_ref, L_CMP)
    o_ref[0] = ev.astype(BF16)
    o_ref[1] = od.astype(BF16)


def _compress_prompt(rows, pos2, w1p, w2p):
    B, S, _ = rows.shape
    nchunk = S // CMP_ROWS
    full = lambda a: pl.BlockSpec(a.shape, lambda b, i: (0,) * a.ndim)
    return pl.pallas_call(
        _compress_kernel, name="compress", grid=(B, nchunk),
        in_specs=[pl.BlockSpec((None, CMP_ROWS, KVD), lambda b, i: (b, i, 0)), full(pos2), full(w1p), full(w2p)],
        out_specs=pl.BlockSpec((None, 2, 128, KVD), lambda b, i: (b, 0, i, 0)),
        out_shape=jax.ShapeDtypeStruct((B, 2, nchunk * 128, KVD), BF16),
        scratch_shapes=[pltpu.VMEM((256, KVD), F32)],
        compiler_params=_cparams(("arbitrary", "arbitrary")),
    )(rows, pos2, w1p, w2p)


def _compress_pages_kernel(pt_ref, pool_ref, pos_ref, w1_ref, w2_ref, o_ref, buf, nat, tmp_ref, sem, *, n_pages, nb):
    b = pl.program_id(0)
    per_page = PAGE_SIZE // L_CMP

    def copy(bb, p, slot):
        page = pt_ref[bb * n_pages + p]
        return pltpu.make_async_copy(pool_ref.at[page], buf.at[slot, pl.ds(PAGE_SIZE * p, PAGE_SIZE), :],
                                     sem.at[slot])

    def start_all(bb, slot):
        for p in range(n_pages):
            copy(bb, p, slot).start()

    def wait_all(bb, slot):
        for p in range(n_pages):
            copy(bb, p, slot).wait()

    def transpose_pages(slot):
        for p in range(n_pages):
            t = buf[slot, pl.ds(PAGE_SIZE * p, PAGE_SIZE), :].T
            for n in range(per_page):
                nat[slot, pl.ds((per_page * p + n) * BLOCK_PITCH, L_CMP), :] = t[n * L_CMP:(n + 1) * L_CMP, :]

    slot = lax.rem(b, 2)

    @pl.when(b == 0)
    def _():
        start_all(0, 0)
        if nb > 1:
            start_all(1, 1)
        wait_all(0, 0)
        transpose_pages(0)

    @pl.when(b + 2 < nb)
    def _():
        start_all(b + 2, slot)

    @pl.when(b + 1 < nb)
    def _():
        wait_all(b + 1, 1 - slot)

    def steady(cur):
        if nb > 1:
            transpose_pages(1 - cur)
        ev, od = _compress_rows(nat.at[cur], pos_ref, w1_ref, w2_ref, tmp_ref, BLOCK_PITCH)
        o_ref[0] = ev.astype(BF16)
        o_ref[1] = od.astype(BF16)

    @pl.when(slot == 0)
    def _():
        steady(0)

    @pl.when(slot == 1)
    def _():
        steady(1)


def _compress_pages(page_table, pool_t, pos2, w1p, w2p):
    DB, n_pages = page_table.shape
    assert n_pages * PAGE_SIZE == CMP_ROWS and KVD == PAGE_SIZE
    full = lambda a: pl.BlockSpec(a.shape, lambda b, pt: (0,) * a.ndim)
    return pl.pallas_call(
        functools.partial(_compress_pages_kernel, n_pages=n_pages, nb=DB), name="compress_pages",
        grid_spec=pltpu.PrefetchScalarGridSpec(
            num_scalar_prefetch=1, grid=(DB,),
            in_specs=[pl.BlockSpec(memory_space=pl.ANY), full(pos2), full(w1p), full(w2p)],
            out_specs=pl.BlockSpec((None, 2, 128, KVD), lambda b, pt: (b, 0, 0, 0)),
            scratch_shapes=[pltpu.VMEM((2, CMP_ROWS, KVD), F32), pltpu.VMEM((2, 256 * BLOCK_PITCH, KVD), F32),
                            pltpu.VMEM((256, KVD), F32), pltpu.SemaphoreType.DMA((2,))]),
        out_shape=jax.ShapeDtypeStruct((DB, 2, 128, KVD), BF16),
        compiler_params=_cparams(("arbitrary",)),
    )(page_table.reshape(-1), pool_t, pos2, w1p, w2p)


def _topk_extract(v, k, want_idx, axis=1):
    pos = lax.broadcasted_iota(jnp.int32, v.shape, axis).astype(F32)
    idx = []
    for _ in range(k):
        mx = jnp.max(v, axis=axis, keepdims=True)
        first = jnp.min(jnp.where(v == mx, pos, 1e9), axis=axis, keepdims=True)
        v = jnp.where(pos == first, -jnp.inf, v)
        if want_idx:
            idx.append(first)
    return v, idx


def _cmp_kernel(q_ref, kc_ref, vc_ref, gates_ref, nb_ref, oc_ref, m_ref, *, nc):
    qt = pl.program_id(1)
    t0 = qt * TQ
    half = nc // 2
    gates = gates_ref[...]

    def body(w):
        rows = lambda ref: jnp.concatenate([ref[0:w, :], ref[half:half + w, :]], axis=0)
        lane = lax.broadcasted_iota(jnp.int32, (2 * w, LANES), 1)
        crow = lax.broadcasted_iota(jnp.int32, (2 * w, LANES), 0)
        cnat = jnp.where(crow < w, 2 * crow, 2 * (crow - w) + 1)
        cbase = t0 // L_CMP - 8
        e = jnp.where(lane < 16, lane, lane - 16)
        ext = jnp.where((lane < 32) & (cnat == cbase + e), 1.0, 0.0)
        ext = jnp.where((lane == 32) & (cnat >= cbase + 16), NEG, ext).astype(BF16)
        lo = lane < HEAD_DIM
        kc = rows(kc_ref)
        vc = rows(vc_ref)
        zero = jnp.zeros_like(kc)
        rhs = [jnp.concatenate([jnp.where(lo, kc, zero), ext], axis=1),
               jnp.concatenate([jnp.where(lo, zero, kc), ext], axis=1)]
        vg = [jnp.where(lo, vc, zero), jnp.where(lo, zero, vc)]
        imp = [jnp.zeros((TQ, 2 * w), F32), jnp.zeros((TQ, 2 * w), F32)]

        def scores(hh):
            qs = q_ref[:, (hh // 2) * LANES:(hh // 2 + 1) * LANES]
            return _dot_nt(jnp.concatenate([qs, nb_ref[hh]], axis=1), rhs[hh % 2])

        s_next = scores(0)
        osl = None
        for hh in range(N_HEADS):
            g = hh % 2
            s = s_next
            if hh + 1 < N_HEADS:
                s_next = scores(hh + 1)
            m = jnp.max(s, axis=-1, keepdims=True)
            ex = jnp.exp2(s - m)
            l = jnp.sum(ex, axis=-1, keepdims=True)
            p = ex * jnp.where(m > 0.1 * NEG, 1.0 / l, 0.0)
            imp[g] = imp[g] + p
            o = gates[:, hh:hh + 1] * _dot(p.astype(BF16), vg[g])
            if g == 0:
                osl = o
            else:
                oc_ref[:, (hh // 2) * LANES:(hh // 2 + 1) * LANES] = (osl + o).astype(BF16)
        blk = lax.broadcasted_iota(jnp.int32, (w, TQ), 0)
        tcol = t0 + lax.broadcasted_iota(jnp.int32, (w, TQ), 1)
        forced = (blk == tcol // L_SEL) | (blk == 0)
        started = blk * L_SEL <= tcol
        for g in range(N_KV):
            v = (imp[g][:, :w] + imp[g][:, w:]).T
            v = jnp.where(forced, FORCED_SCORE, jnp.where(started, v, -1.0))
            v, _ = _topk_extract(v, N_SEL, False, axis=0)
            m_ref[:, g * half:g * half + w] = jnp.where(v == -jnp.inf, 0.0, NEG).T.astype(BF16)
            if w < half:
                m_ref[:, g * half + w:(g + 1) * half] = jnp.full((TQ, half - w), NEG, BF16)

    w_short = half // 2
    if w_short % LANES == 0 and w_short >= N_SEL:
        n_short = w_short * L_SEL // TQ

        @pl.when(qt < n_short)
        def _():
            body(w_short)

        @pl.when(qt >= n_short)
        def _():
            body(half)
    else:
        body(half)


def _cmp_call(q, kc, vc, gates, nbt):
    B, S, _ = q.shape
    nc = kc.shape[1]
    row = lambda w: pl.BlockSpec((None, TQ, w), lambda b, t: (b, t, 0))
    perb = lambda a: pl.BlockSpec((None,) + a.shape[1:], lambda b, t: (b,) + (0,) * (a.ndim - 1))
    full = lambda a: pl.BlockSpec(a.shape, lambda b, t: (0,) * a.ndim)
    return pl.pallas_call(
        functools.partial(_cmp_kernel, nc=nc), name="cmp", grid=(B, S // TQ),
        in_specs=[row(QD), perb(kc), perb(vc), row(LANES), full(nbt)],
        out_specs=[row(QD), row(nc)],
        out_shape=[jax.ShapeDtypeStruct((B, S, QD), BF16), jax.ShapeDtypeStruct((B, S, nc), BF16)],
        compiler_params=_cparams(("arbitrary", "arbitrary")),
    )(q, kc, vc, gates, nbt)


def _sel_kernel(q_ref, m_ref, gates_ref, k_ref, v_ref, oh_ref, bt_ref, os_ref, lhs_s, acc_s, m_s, *, nh):
    qt = pl.program_id(1)
    tph = HALF_KEYS // KT
    for j in range(GROUP):
        qs = q_ref[:, j * LANES:(j + 1) * LANES]
        for h in range(nh):
            for g in range(N_KV):
                c0 = (g * nh + h) * LANES
                lhs_s[(h * 2 + g) * GROUP + j] = jnp.concatenate([qs, m_ref[:, c0:c0 + LANES]], axis=1)
    lo_q = lax.broadcasted_iota(jnp.int32, (TQ, LANES), 1) < HEAD_DIM

    def step(kt, nk, kind):
        rows = nk * KT
        k_t = k_ref[pl.ds(pl.multiple_of(kt * KT, KT), rows), :]
        v_t = v_ref[pl.ds(pl.multiple_of(kt * KT, KT), rows), :]
        h = kt // tph
        oh = oh_ref[pl.ds(pl.multiple_of((kt - h * tph) * KT, KT), rows), :]
        lane = lax.broadcasted_iota(jnp.int32, (rows, LANES), 1)
        lo_k = lane < HEAD_DIM
        zero = jnp.zeros_like(k_t)
        rhs = [jnp.concatenate([jnp.where(lo_k, k_t, zero), oh], axis=1),
               jnp.concatenate([jnp.where(lo_k, zero, k_t), oh], axis=1)]
        vg = [jnp.where(lo_k, v_t, jnp.where(lane == HEAD_DIM, 1.0, 0.0).astype(BF16)),
              jnp.where(lo_k, jnp.where(lane == 0, 1.0, 0.0).astype(BF16), v_t)]
        def scores(hh):
            g = hh % 2
            return _dot_nt(lhs_s[(h * 2 + g) * GROUP + hh // 2], rhs[g])

        s_next = scores(0)
        for hh in range(N_HEADS):
            s = s_next
            if hh + 1 < N_HEADS:
                s_next = scores(hh + 1)
            if kind:
                near, diag = bt_ref[0, hh], bt_ref[1, hh]
                s00, s01, s10, s11 = s[:SUB, :SUB], s[:SUB, SUB:], s[SUB:, :SUB], s[SUB:, SUB:]
                if kind == 1:
                    top, bot = [s00, s01 + near], [s10, s11]
                else:
                    top, bot = [s00 + diag, jnp.full((SUB, SUB), NEG, F32)], [s10 + near, s11 + diag]
                s = jnp.concatenate([jnp.concatenate(top, axis=1), jnp.concatenate(bot, axis=1)], axis=0)
            mc = jnp.broadcast_to(jnp.max(s, axis=-1, keepdims=True), (TQ, LANES))
            if kind == 2:
                mn = mc
            else:
                mp = m_s[hh]
                mn = jnp.maximum(mp, mc)
            m_s[hh] = mn
            p = jnp.exp2(s - jnp.concatenate([mn] * (rows // LANES), axis=1))
            pv = _dot(p.astype(BF16), vg[hh % 2])
            acc_s[hh] = pv if kind == 2 else acc_s[hh] * jnp.exp2(mp - mn) + pv

    step(qt, 1, 2)

    @pl.when(qt >= 1)
    def _():
        step(qt - 1, 1, 1)

    nfar = jnp.maximum(qt - 1, 0)
    nbig = nfar // FAR_TILES

    def big(i, c):
        step(i * FAR_TILES, FAR_TILES, False)
        return c

    def small(kt, c):
        step(kt, 1, False)
        return c

    lax.fori_loop(0, nbig, big, 0)
    lax.fori_loop(nbig * FAR_TILES, nfar, small, 0)

    gates = gates_ref[...]
    for j in range(GROUP):
        a0 = acc_s[2 * j]
        a1 = acc_s[2 * j + 1]
        c = N_HEADS + 2 * j
        sc0 = gates[:, c:c + 1] / a0[:, HEAD_DIM:HEAD_DIM + 1]
        sc1 = gates[:, c + 1:c + 2] / a1[:, 0:1]
        os_ref[:, j * LANES:(j + 1) * LANES] = jnp.where(lo_q, a0 * sc0, a1 * sc1).astype(BF16)


def _sel_call(q, msk, gates, kb, vb, onehot, btab):
    B, S, _ = q.shape
    nh = msk.shape[2] // (2 * LANES)
    row = lambda w: pl.BlockSpec((None, TQ, w), lambda b, t: (b, t, 0))
    perb = lambda a: pl.BlockSpec((None,) + a.shape[1:], lambda b, t: (b,) + (0,) * (a.ndim - 1))
    full = lambda a: pl.BlockSpec(a.shape, lambda b, t: (0,) * a.ndim)
    return pl.pallas_call(
        functools.partial(_sel_kernel, nh=nh), name="sel", grid=(B, S // TQ),
        in_specs=[row(QD), row(msk.shape[2]), row(LANES), perb(kb), perb(vb), full(onehot), full(btab)],
        out_specs=row(QD),
        out_shape=jax.ShapeDtypeStruct((B, S, QD), BF16),
        scratch_shapes=[pltpu.VMEM((nh * 2 * GROUP, TQ, 2 * LANES), BF16), pltpu.VMEM((N_HEADS, TQ, LANES), F32),
                        pltpu.VMEM((N_HEADS, TQ, LANES), F32)],
        compiler_params=_cparams(("arbitrary", "arbitrary")),
    )(q, msk, gates, kb, vb, onehot, btab)


def _win_kernel(q_ref, gates_ref, k0_ref, k1_ref, k2_ref, v0_ref, v1_ref, v2_ref, wt_ref, ow_ref):
    qt = pl.program_id(1)
    lo_k = lax.broadcasted_iota(jnp.int32, (3 * KT, LANES), 1) < HEAD_DIM
    kcat = jnp.concatenate([k0_ref[...], k1_ref[...], k2_ref[...]], axis=0)
    vcat = jnp.concatenate([v0_ref[...], v1_ref[...], v2_ref[...]], axis=0)
    lane_k = lax.broadcasted_iota(jnp.int32, (3 * KT, LANES), 1)
    zero = jnp.zeros_like(kcat)
    kg = [jnp.where(lo_k, kcat, zero), jnp.where(lo_k, zero, kcat)]
    vg = [jnp.where(lo_k, vcat, jnp.where(lane_k == HEAD_DIM, 1.0, 0.0).astype(BF16)),
          jnp.where(lo_k, jnp.where(lane_k == 0, 1.0, 0.0).astype(BF16), vcat)]
    lo_q = lax.broadcasted_iota(jnp.int32, (TQ, LANES), 1) < HEAD_DIM
    gates = gates_ref[...]
    ii = lax.broadcasted_iota(jnp.int32, (SUB, SUB), 0)
    jj = lax.broadcasted_iota(jnp.int32, (SUB, SUB), 1)
    far_edge = jnp.where(jj >= ii, 0.0, NEG)
    outside = jnp.full((SUB, SUB), NEG, F32)

    def scores(hh):
        return _dot_nt(q_ref[:, (hh // 2) * LANES:(hh // 2 + 1) * LANES], kg[hh % 2])

    def body(edge):
        if edge:
            lane = lax.broadcasted_iota(jnp.int32, (1, 3 * KT), 1)
            pen = jnp.where(lane < KT, jnp.where(qt >= 2, 0.0, NEG),
                            jnp.where(lane < 2 * KT, jnp.where(qt >= 1, 0.0, NEG), 0.0))
        s_next = scores(0)
        pv = [None, None]
        for hh in range(N_HEADS):
            j, g = hh // 2, hh % 2
            s = s_next
            if hh + 1 < N_HEADS:
                s_next = scores(hh + 1)
            near, diag = wt_ref[0, hh], wt_ref[1, hh]
            col = lambda a, c: s[a * SUB:(a + 1) * SUB, c * SUB:(c + 1) * SUB]
            top = [col(0, 0) + far_edge, col(0, 1), col(0, 2), col(0, 3) + near, col(0, 4) + diag, outside]
            bot = [outside, col(1, 1) + far_edge, col(1, 2), col(1, 3), col(1, 4) + near, col(1, 5) + diag]
            s = jnp.concatenate([jnp.concatenate(top, axis=1), jnp.concatenate(bot, axis=1)], axis=0)
            if edge:
                s = s + pen
            ex = jnp.exp2(s - jnp.max(s, axis=-1, keepdims=True))
            pv[g] = _dot(ex.astype(BF16), vg[g])
            if g == 1:
                c = 2 * N_HEADS + 2 * j
                sc0 = gates[:, c:c + 1] / pv[0][:, HEAD_DIM:HEAD_DIM + 1]
                sc1 = gates[:, c + 1:c + 2] / pv[1][:, 0:1]
                ow_ref[:, j * LANES:(j + 1) * LANES] = jnp.where(lo_q, pv[0] * sc0, pv[1] * sc1).astype(BF16)

    @pl.when(qt >= 2)
    def _():
        body(False)

    @pl.when(qt < 2)
    def _():
        body(True)


def _win_call(q, gates, kb, vb, wtab):
    B, S, _ = q.shape
    row = lambda w: pl.BlockSpec((None, TQ, w), lambda b, t: (b, t, 0))
    back = lambda d: pl.BlockSpec((None, KT, KVD), lambda b, t: (b, jnp.maximum(t - d, 0), 0))
    full = lambda a: pl.BlockSpec(a.shape, lambda b, t: (0,) * a.ndim)
    return pl.pallas_call(
        _win_kernel, name="win", grid=(B, S // TQ),
        in_specs=[row(QD), row(LANES), back(2), back(1), back(0), back(2), back(1), back(0), full(wtab)],
        out_specs=row(QD),
        out_shape=jax.ShapeDtypeStruct((B, S, QD), BF16),
        compiler_params=_cparams(("arbitrary", "arbitrary")),
    )(q, gates, kb, kb, kb, vb, vb, vb, wtab)


def _merge_kernel(za_ref, oc_ref, os_ref, ow_ref, ga_ref, gb_ref, x_ref, wuc_ref, wua_ref, wo_ref, g_ref, b_ref,
                  h_ref):
    a = _dot(za_ref[...], wuc_ref[...])
    ob = oc_ref[...].astype(F32) + os_ref[...].astype(F32) + ow_ref[...].astype(F32)
    bm = _dot(ob.astype(BF16), wua_ref[...])
    pre = ga_ref[...].astype(F32) * a + gb_ref[...].astype(F32) * bm
    mix = _dot(pre.astype(BF16), wo_ref[...])
    h_ref[...] = _layer_norm(ALPHA * x_ref[...] + mix, g_ref[...], b_ref[...])


def _merge_call(za, oc, os_, ow, ga, gb, x, wuc, wua, wo, g, b, tm):
    R = x.shape[0]
    row = lambda w: pl.BlockSpec((tm, w), lambda i: (i, 0))
    full = lambda a: pl.BlockSpec(a.shape, lambda i: (0,) * a.ndim)
    return pl.pallas_call(
        _merge_kernel, name="merge", grid=(R // tm,),
        in_specs=[row(D_CONV), row(QD), row(QD), row(QD), row(D_MODEL), row(D_MODEL), row(D_MODEL),
                  full(wuc), full(wua), full(wo), full(g), full(b)],
        out_specs=row(D_MODEL), out_shape=jax.ShapeDtypeStruct((R, D_MODEL), F32),
        compiler_params=_cparams(("arbitrary",)),
    )(za, oc, os_, ow, ga, gb, x, wuc, wua, wo, g, b)


def _ffn_kernel(h_ref, wg_ref, wu_ref, wd_ref, g_ref, b_ref, y_ref):
    h = h_ref[...]
    hb = h.astype(BF16)
    act = jax.nn.silu(_dot(hb, wg_ref[...])) * _dot(hb, wu_ref[...])
    f = _dot(act.astype(BF16), wd_ref[...])
    y_ref[...] = _layer_norm(ALPHA * h + f, g_ref[...], b_ref[...])


def _ffn_call(h, wg, wu, wd, g, b, tm):
    R = h.shape[0]
    row = lambda w: pl.BlockSpec((tm, w), lambda i: (i, 0))
    full = lambda a: pl.BlockSpec(a.shape, lambda i: (0,) * a.ndim)
    return pl.pallas_call(
        _ffn_kernel, name="ffn", grid=(R // tm,),
        in_specs=[row(D_MODEL), full(wg), full(wu), full(wd), full(g), full(b)],
        out_specs=row(D_MODEL), out_shape=jax.ShapeDtypeStruct((R, D_MODEL), F32),
        compiler_params=_cparams(("arbitrary",)),
    )(h, wg, wu, wd, g, b)


def _scmp_kernel(qh_ref, kc_ref, vc_ref, gate_ref, bc_ref, oc_ref, imp_ref, *, nc):
    half = nc // 2
    s = _dot_nt(qh_ref[...], kc_ref[...]) + bc_ref[...]
    m = jnp.max(s, axis=-1, keepdims=True)
    ex = jnp.exp2(s - m)
    p = ex * (1.0 / jnp.sum(ex, axis=-1, keepdims=True))
    o = _dot(p.astype(BF16), vc_ref[...]) * gate_ref[...]
    oc_ref[...] = o[0:GROUP] + o[GROUP:2 * GROUP]
    for g in range(N_KV):
        pg = jnp.sum(p[g * GROUP:(g + 1) * GROUP], axis=0, keepdims=True)
        imp_ref[g:g + 1, :] = pg[:, :half] + pg[:, half:]


def _scmp_call(qh, kc, vc, gate_c, bc):
    DB, _, nc, _ = (kc.shape[0], None, kc.shape[1], None)
    perb = lambda a: pl.BlockSpec((None,) + a.shape[1:], lambda b: (b,) + (0,) * (a.ndim - 1))
    full = lambda a: pl.BlockSpec(a.shape, lambda b: (0,) * a.ndim)
    return pl.pallas_call(
        functools.partial(_scmp_kernel, nc=nc), name="sample_cmp", grid=(DB,),
        in_specs=[perb(qh), perb(kc), perb(vc), perb(gate_c), full(bc)],
        out_specs=[pl.BlockSpec((None, GROUP, LANES), lambda b: (b, 0, 0)),
                   pl.BlockSpec((None, N_KV, nc // 2), lambda b: (b, 0, 0))],
        out_shape=[jax.ShapeDtypeStruct((DB, GROUP, LANES), F32), jax.ShapeDtypeStruct((DB, N_KV, nc // 2), F32)],
        compiler_params=_cparams(("arbitrary",)),
    )(qh, kc, vc, gate_c, bc)


def _stopk_kernel(imp_ref, idx_ref, *, k):
    v = imp_ref[...]
    lane = lax.broadcasted_iota(jnp.int32, v.shape, 1)
    v = jnp.where(lane == 0, FORCED_SCORE, v)
    _, idx = _topk_extract(v, k, True)
    out = jnp.zeros(v.shape, F32)
    for i, col in enumerate(idx):
        out = jnp.where(lane == i, col, out)
    idx_ref[...] = out.astype(jnp.int32)


def _stopk_call(imp, k):
    return pl.pallas_call(
        functools.partial(_stopk_kernel, k=k), name="sample_topk",
        out_shape=jax.ShapeDtypeStruct(imp.shape, jnp.int32),
        compiler_params=pltpu.CompilerParams(vmem_limit_bytes=VMEM_LIMIT),
    )(imp)


def _ssel_kernel(idx_ref, pt_ref, q_ref, knew_ref, vnew_ref, kwnew_ref, vwnew_ref, kwcol_ref, vwcol_ref,
                 gate_s_ref, gate_w_ref, kwin_ref, vwin_ref, tn_ref, tw_ref, b0_ref, kpool_ref, vpool_ref,
                 os_ref, ow_ref, kwo_ref, vwo_ref, kbuf, vbuf, sem, *, n_pages, nsel, n_last):
    b = pl.program_id(0)
    nb = pl.num_programs(0)
    slot = lax.rem(b, 2)

    def copies(bb, sl, g, s):
        n = idx_ref[(bb * N_KV + g) * nsel + s]
        page = pt_ref[bb * n_pages + n // 2]
        dst = pl.ds(s * PAGE_SIZE, PAGE_SIZE)
        return (pltpu.make_async_copy(kpool_ref.at[page, g], kbuf.at[sl, g, :, dst], sem.at[sl, 0]),
                pltpu.make_async_copy(vpool_ref.at[page, g], vbuf.at[sl, g, :, dst], sem.at[sl, 1]))

    def start_all(bb, sl):
        for g in range(N_KV):
            for s in range(nsel):
                ck, cv = copies(bb, sl, g, s)
                ck.start()
                cv.start()

    @pl.when(b == 0)
    def _():
        start_all(b, 0)

    @pl.when(b + 1 < nb)
    def _():
        start_all(b + 1, 1 - slot)

    def attend(qg, s, k_new, v_new, vt, bias_new, gate):
        s_new = jnp.sum(qg.astype(F32) * k_new.astype(BF16).astype(F32), axis=-1, keepdims=True) + bias_new
        m = jnp.maximum(jnp.max(s, axis=-1, keepdims=True), s_new)
        ex = jnp.exp2(s - m)
        ex_new = jnp.exp2(s_new - m)
        l = jnp.sum(ex, axis=-1, keepdims=True) + ex_new
        o = _dot_nt(ex.astype(BF16), vt.astype(BF16)) + ex_new.astype(BF16).astype(F32) * v_new.astype(BF16).astype(F32)
        return o * (gate / l)

    wb = kwin_ref.shape[-1]
    lane_w = lax.broadcasted_iota(jnp.int32, (HEAD_DIM, wb), 1)
    for g in range(N_KV):
        qg = q_ref[g]
        kw = kwin_ref[g]
        vw = vwin_ref[g]
        s = _dot(qg, kw.astype(BF16)) + tw_ref[g]
        ow_ref[g] = attend(qg, s, kwnew_ref[g], vwnew_ref[g], vw, b0_ref[g], gate_w_ref[g])
        kwo_ref[g] = jnp.where(lane_w == wb - 1, kwcol_ref[g], pltpu.roll(kw, wb - 1, 1))
        vwo_ref[g] = jnp.where(lane_w == wb - 1, vwcol_ref[g], pltpu.roll(vw, wb - 1, 1))

    for g in range(N_KV):
        for s in range(nsel):
            ck, cv = copies(b, slot, g, s)
            ck.wait()
            cv.wait()

    upper = lax.broadcasted_iota(jnp.int32, (GROUP, PAGE_SIZE), 1) >= L_SEL
    keep_upper = jnp.where(upper, 0.0, NEG)
    keep_lower = jnp.where(upper, NEG, 0.0)
    for g in range(N_KV):
        qg = q_ref[g]
        pieces = []
        for s in range(nsel):
            n = idx_ref[(b * N_KV + g) * nsel + s]
            piece = jnp.where(n % 2 == 1, keep_upper, keep_lower)
            piece = piece + jnp.where(n == n_last, 1.0, 0.0) * tn_ref[0, g]
            piece = piece + jnp.where(n == n_last - 1, 1.0, 0.0) * tn_ref[1, g]
            pieces.append(piece)
        s = _dot(qg, kbuf[slot, g].astype(BF16)) + jnp.concatenate(pieces, axis=1)
        os_ref[g] = attend(qg, s, knew_ref[g], vnew_ref[g], vbuf[slot, g], b0_ref[g], gate_s_ref[g])


def _ssel_call(idx, page_table, qg, knew, vnew, kwnew, vwnew, kwcol, vwcol, gate_s, gate_w, kwin, vwin, tn, tw, b0,
               kpool, vpool):
    DB, n_pages = page_table.shape
    nsel = idx.shape[-1]
    wb = kwin.shape[-1]
    n_last = n_pages * (PAGE_SIZE // L_SEL) - 1
    perb = lambda a: pl.BlockSpec((None,) + a.shape[1:], lambda b, *_: (b,) + (0,) * (a.ndim - 1))
    full = lambda a: pl.BlockSpec(a.shape, lambda b, *_: (0,) * a.ndim)
    anyspec = pl.BlockSpec(memory_space=pl.ANY)
    head_o = pl.BlockSpec((None, N_KV, GROUP, HEAD_DIM), lambda b, *_: (b, 0, 0, 0))
    win_o = pl.BlockSpec((None, N_KV, HEAD_DIM, wb), lambda b, *_: (b, 0, 0, 0))
    head_s = jax.ShapeDtypeStruct((DB, N_KV, GROUP, HEAD_DIM), F32)
    win_s = jax.ShapeDtypeStruct((DB, N_KV, HEAD_DIM, wb), F32)
    return pl.pallas_call(
        functools.partial(_ssel_kernel, n_pages=n_pages, nsel=nsel, n_last=n_last), name="sample_sel_win",
        grid_spec=pltpu.PrefetchScalarGridSpec(
            num_scalar_prefetch=2, grid=(DB,),
            in_specs=[perb(qg), perb(knew), perb(vnew), perb(kwnew), perb(vwnew), perb(kwcol), perb(vwcol),
                      perb(gate_s), perb(gate_w), perb(kwin), perb(vwin), full(tn), full(tw), full(b0),
                      anyspec, anyspec],
            out_specs=[head_o, head_o, win_o, win_o],
            scratch_shapes=[pltpu.VMEM((2, N_KV, HEAD_DIM, nsel * PAGE_SIZE), F32),
                            pltpu.VMEM((2, N_KV, HEAD_DIM, nsel * PAGE_SIZE), F32),
                            pltpu.SemaphoreType.DMA((2, 2))]),
        out_shape=[head_s, head_s, win_s, win_s],
        compiler_params=_cparams(("arbitrary",)),
    )(idx.reshape(-1), page_table.reshape(-1), qg, knew, vnew, kwnew, vwnew, kwcol, vwcol, gate_s, gate_w,
      kwin, vwin, tn, tw, b0, kpool, vpool)


def _perm_w_in(w_in):
    o_q = 3 * D_CONV
    o_kv = o_q + QD
    o_ng = o_kv + 6 * KVD
    o_mg = o_ng + 3 * N_HEADS
    q = w_in[:, o_q:o_kv].reshape(D_MODEL, N_KV, GROUP, HEAD_DIM).transpose(0, 2, 1, 3).reshape(D_MODEL, QD)
    q = q * (HEAD_DIM ** -0.5 * LOG2E)
    ng = w_in[:, o_ng:o_mg].reshape(D_MODEL, N_KV, GROUP, 3).transpose(0, 3, 2, 1).reshape(D_MODEL, 3 * N_HEADS)
    ng = jnp.pad(ng, ((0, 0), (0, LANES - 3 * N_HEADS)))
    return jnp.concatenate([w_in[:, :o_q], q, w_in[:, o_kv:o_ng], ng, w_in[:, o_mg:]], axis=1).astype(BF16)


def _perm_heads(t):
    return t.reshape(t.shape[:-1] + (N_KV, GROUP)).swapaxes(-1, -2).reshape(t.shape)


def _cmp_weights(w1, w2, pos):
    eye = jnp.eye(N_KV, dtype=F32)
    w1p = jnp.einsum('ldh,ab->ladbh', w1.reshape(L_CMP, HEAD_DIM, CMP_HID), eye)
    w1p = w1p.reshape(L_CMP // 2, 2 * KVD, N_KV * CMP_HID).astype(BF16)
    w2p = jnp.einsum('hd,ab->ahbd', w2, eye).reshape(N_KV * CMP_HID, KVD).astype(BF16)
    pos2 = jnp.tile(pos[:, None, :], (1, N_KV, 1)).reshape(L_CMP, 1, KVD)
    return pos2, w1p, w2p


def kernel(x_prompt, x_sample, cache_k_cmp, cache_v_cmp, cache_k_slc, cache_v_slc, state_k_win, state_v_win,
           state_conv, page_table, w_in, conv_w, w_cmp_k1, w_cmp_k2, pos_cmp_k, w_cmp_v1, w_cmp_v2, pos_cmp_v,
           rel_bias, w_up_conv, w_up_attn, w_o, ln1_g, ln1_b, w_gate_ffn, w_up_ffn, w_down_ffn, ln2_g, ln2_b):
    B, S, _ = x_prompt.shape
    DB = x_sample.shape[0]
    n_pages = page_table.shape[1]
    P = n_pages * PAGE_SIZE
    n_pool = cache_k_cmp.shape[1]
    wb = state_k_win.shape[2]
    assert x_sample.shape[1] == 1 and w_in.shape[0] == 1
    assert S % HALF_KEYS == 0 and P == HALF_KEYS and wb == WINDOW and S >= WINDOW
    nc = S // L_CMP
    nh = S // HALF_KEYS

    w_perm = _perm_w_in(w_in[0])
    cw = conv_w[0]
    cmp_k = _cmp_weights(w_cmp_k1[0], w_cmp_k2[0], pos_cmp_k[0])
    cmp_v = _cmp_weights(w_cmp_v1[0], w_cmp_v2[0], pos_cmp_v[0])
    wuc = w_up_conv[0].astype(BF16)
    wua = w_up_attn[0].reshape(N_KV, GROUP, HEAD_DIM, D_MODEL).transpose(1, 0, 2, 3).reshape(QD, D_MODEL).astype(BF16)
    wo = w_o[0].astype(BF16)
    wg, wu, wd = w_gate_ffn[0].astype(BF16), w_up_ffn[0].astype(BF16), w_down_ffn[0].astype(BF16)
    g1, b1, g2, b2 = ln1_g, ln1_b, ln2_g, ln2_b

    bucket = _bucket_table()
    bias_p = (rel_bias[bucket] - rel_bias[N_BUCKETS - 1][None, :]) * LOG2E
    bias_pp = _perm_heads(bias_p)
    ii = np.arange(TQ)[:, None]
    dist_c = ii - L_CMP * np.arange(16)[None, :] + 8 * L_CMP - (L_CMP - 1)
    nb = _bias_from_dist(bias_pp, dist_c, True)
    nb_hi = nb.astype(BF16)
    nb_lo = jnp.where(jnp.asarray(dist_c < 0), 0.0, nb - nb_hi.astype(F32)).astype(BF16)
    nbt = jnp.concatenate([nb_hi, nb_lo, jnp.ones((N_HEADS, TQ, 1), BF16),
                           jnp.zeros((N_HEADS, TQ, LANES - 33), BF16)], axis=-1)
    wtab =jnp.stack([_toeplitz(bias_pp, SUB, False, SUB), _toeplitz(bias_pp, 0, True, SUB)])
    onehot = jnp.asarray((np.arange(HALF_KEYS)[:, None] // L_SEL == np.arange(LANES)[None, :]), BF16)

    (za, q, kcm, vcm, ksl, vsl, kwn, vwn, kslb, vslb, kwnb, vwnb, gates, ga, gb, conv_p) = _proj_call(
        x_prompt, w_perm, cw, None, 256)
    kc = _compress_prompt(kcm, *cmp_k).reshape(B, nc, KVD)
    vc = _compress_prompt(vcm, *cmp_v).reshape(B, nc, KVD)
    oc, msk = _cmp_call(q, kc, vc, gates, nbt)
    os_ = _sel_call(q, msk, gates, kslb, vslb, onehot, wtab)
    ow = _win_call(q, gates, kwnb, vwnb, wtab)
    f2 = lambda t: t.reshape(B * S, t.shape[-1])
    h = _merge_call(f2(za), f2(oc), f2(os_), f2(ow), f2(ga), f2(gb), f2(x_prompt), wuc, wua, wo, g1, b1, 256)
    y_p = _ffn_call(h, wg, wu, wd, g2, b2, 256).reshape(B, S, D_MODEL)
    kv5 = lambda t: t.reshape(1, B, S, N_KV, HEAD_DIM)
    keep = min(WINDOW, S)
    prompt_state = (kv5(kcm), kv5(vcm), kv5(ksl), kv5(vsl), kv5(kwn)[:, :, S - keep:], kv5(vwn)[:, :, S - keep:],
                    conv_p[None])

    xs = x_sample.reshape(1, DB, D_MODEL)
    p0 = state_conv[0, :, 0, :][None]
    p1 = state_conv[0, :, 1, :][None]
    (za_s, q_s, kcm_s, vcm_s, ksl_s, vsl_s, kwn_s, vwn_s, gates_s, ga_s, gb_s, u_s) = _proj_call(
        xs, w_perm, cw, (p0, p1), DB)
    pool_rows = lambda c: jnp.transpose(c[0], (0, 2, 3, 1)).reshape(n_pool, KVD, PAGE_SIZE)
    kc_s = _compress_pages(page_table, pool_rows(cache_k_cmp), *cmp_k).reshape(DB, P // L_CMP, KVD)
    vc_s = _compress_pages(page_table, pool_rows(cache_v_cmp), *cmp_v).reshape(DB, P // L_CMP, KVD)
    lane_lo = jnp.arange(LANES) < HEAD_DIM
    q3 = q_s.reshape(DB, GROUP, LANES)
    qh = jnp.concatenate([jnp.where(lane_lo, q3, 0), jnp.where(lane_lo, 0, q3)], axis=1)
    gsm = gates_s.reshape(DB, LANES)[:, :3 * N_HEADS].reshape(DB, 3, GROUP, N_KV)
    gsm = gsm.transpose(1, 0, 3, 2).reshape(3, DB, N_HEADS, 1)
    row_lo = (jnp.arange(N_HEADS) < GROUP)[:, None]
    lane_own = jnp.where(row_lo, lane_lo[None, :], ~lane_lo[None, :])
    gate3 = jnp.where(lane_own, gsm, 0.0)
    ncs = P // L_CMP
    cnat = np.concatenate([np.arange(0, ncs, 2), np.arange(1, ncs, 2)])
    bc = _bias_from_dist(bias_p, P - (L_CMP * cnat + L_CMP - 1), False)
    oc_s, imp_s = _scmp_call(qh, kc_s, vc_s, gate3[0], bc)
    nsel = N_SEL - 1
    idx = _stopk_call(imp_s.reshape(DB * N_KV, ncs // 2), nsel)[:, :nsel]
    hg = lambda t: t.reshape((N_KV, GROUP) + t.shape[1:])
    zpad = jnp.zeros((N_HEADS, L_SEL), F32)
    t_last = _bias_from_dist(bias_p, L_SEL - np.arange(L_SEL), False)
    t_prev = _bias_from_dist(bias_p, 2 * L_SEL - np.arange(L_SEL), False)
    tn = jnp.stack([hg(jnp.concatenate([zpad, t_last], 1)), hg(jnp.concatenate([t_prev, zpad], 1))])
    tw = hg(_bias_from_dist(bias_p, wb - np.arange(wb), False))
    b0 = hg(bias_p[0][:, None])
    qg = q_s.reshape(DB, GROUP, N_KV, HEAD_DIM).transpose(0, 2, 1, 3)
    gate_g = gsm.reshape(3, DB, N_KV, GROUP, 1)
    rowv = lambda t: t.reshape(DB, N_KV, 1, HEAD_DIM)
    colv = lambda t: t.reshape(DB, N_KV, HEAD_DIM, 1)
    tpose = lambda t: jnp.transpose(t[0], (0, 2, 3, 1))
    os_s, ow_s, kw_out, vw_out = _ssel_call(
        idx, page_table, qg, rowv(ksl_s), rowv(vsl_s), rowv(kwn_s), rowv(vwn_s), colv(kwn_s), colv(vwn_s),
        gate_g[1], gate_g[2], tpose(state_k_win), tpose(state_v_win), tn, tw, b0,
        tpose(cache_k_slc), tpose(cache_v_slc))
    fl = lambda t: t.reshape(DB, -1)
    slots = lambda t: t.transpose(0, 2, 1, 3).reshape(DB, QD)
    h_s = _merge_call(fl(za_s), fl(oc_s), slots(os_s), slots(ow_s), fl(ga_s), fl(gb_s), fl(xs), wuc, wua, wo,
                      g1, b1, DB)
    y_s = _ffn_call(h_s, wg, wu, wd, g2, b2, DB).reshape(DB, 1, D_MODEL)
    kv5s = lambda t: t.reshape(1, DB, 1, N_KV, HEAD_DIM)
    conv_s = jnp.stack([state_conv[0, :, 1, :], u_s.reshape(DB, D_CONV)], axis=1)[None]
    untp = lambda t: jnp.transpose(t, (0, 3, 1, 2))[None]
    sample_state = (kv5s(kcm_s), kv5s(vcm_s), kv5s(ksl_s), kv5s(vsl_s), untp(kw_out), untp(vw_out), conv_s)
    return (y_p, y_s) + prompt_state + sample_state
```
